```python
import math
import jax, jax.numpy as jnp
from jax import lax
import numpy as np

D_MODEL = 1024
BATCH = 2
SEQ = 16384
DEPTH = 4

GRID_W = 64
CTX_LEN = 256
N_MIXERS = 4
MOD_CHUNKS = 6
RMS_EPS = 1e-6
ROPE_THETA = 10000.0
Q_BLOCK = 128

POOL_WINDOWS = (2, 4, 8, 16)
POOL_GROUP = D_MODEL // len(POOL_WINDOWS)
GQA_HEAD_DIM = 128
GQA_Q_HEADS = D_MODEL // GQA_HEAD_DIM
GQA_KV_HEADS = 2
GQA_GROUP = GQA_Q_HEADS // GQA_KV_HEADS
CONV_WIDTH = 3
DIFF_HEAD_DIM = 64
DIFF_HEADS = D_MODEL // (2 * DIFF_HEAD_DIM)
N_EXPERTS = 32
TOP_K = 4
D_FF = D_MODEL
SWIGLU_LIMIT = 7.0
SWIGLU_ALPHA = 1.702
EXPERT_BLOCK = 512

kernel_name = "hybrid_interleaved_dit_block"


def rmsnorm(x, g):
    xf = x.astype(jnp.float32)
    xf = xf * lax.rsqrt(jnp.mean(xf * xf, axis=-1, keepdims=True) + RMS_EPS)
    return xf.astype(x.dtype) * g


def axial_rope_tables(row, col, dim):
    n = dim // 4
    freqs = ROPE_THETA ** (-jnp.arange(n, dtype=jnp.float32) / n)
    ang = jnp.concatenate([row[:, None] * freqs, col[:, None] * freqs], axis=-1)
    return jnp.cos(ang), jnp.sin(ang)


def apply_rope(x, cos, sin):
    half = x.shape[-1] // 2
    x1, x2 = x[..., :half], x[..., half:]
    cos = cos.astype(x.dtype)
    sin = sin.astype(x.dtype)
    return jnp.concatenate([x1 * cos - x2 * sin, x2 * cos + x1 * sin], axis=-1)


def sweep_query_blocks(fn, *qs):
    L = qs[0].shape[-2]
    nb = L // Q_BLOCK

    def split(q):
        qb = q.reshape(q.shape[:-2] + (nb, Q_BLOCK, q.shape[-1]))
        return jnp.moveaxis(qb, -3, 0)

    out = lax.map(lambda blk: fn(*blk), tuple(split(q) for q in qs))
    out = jnp.moveaxis(out, 0, -3)
    return out.reshape(out.shape[:-3] + (L, out.shape[-1]))


def pool_seq(h, w_pool, scale):
    b, l, _ = h.shape
    hf = h.astype(jnp.float32)
    cs = jnp.pad(jnp.cumsum(hf, axis=1), ((0, 0), (1, 0), (0, 0)))
    t = jnp.arange(l)
    parts = []
    for g, w in enumerate(POOL_WINDOWS):
        lo = jnp.clip(t - w // 2, 0, l)
        hi = jnp.clip(t + (w - w // 2), 0, l)
        cg = cs[..., g * POOL_GROUP:(g + 1) * POOL_GROUP]
        mean = (cg[:, hi] - cg[:, lo]) / (hi - lo).astype(jnp.float32)[:, None]
        parts.append(mean - hf[..., g * POOL_GROUP:(g + 1) * POOL_GROUP])
    d = jnp.stack(parts, axis=2).astype(h.dtype)
    y = jnp.einsum('blgc,gcd->blgd', d, w_pool).reshape(b, l, D_MODEL)
    return y * scale


def pool_mixer(h_ctx, h_lat, w_pool, scale, need_ctx):
    y_ctx = pool_seq(h_ctx, w_pool, scale) if need_ctx else None
    return y_ctx, pool_seq(h_lat, w_pool, scale)


def gqa_attend(q, k, v):
    s = jnp.einsum('bhgqd,bhkd->bhgqk', q, k)
    p = jax.nn.softmax(s.astype(jnp.float32) * GQA_HEAD_DIM ** -0.5, axis=-1)
    return jnp.einsum('bhgqk,bhkd->bhgqd', p.astype(v.dtype), v)


def gqa_mixer(h_ctx, h_lat, w_qkv, w_o, q_gain, k_gain, cos, sin, need_ctx):
    def project(h):
        b, l, _ = h.shape
        q, k, v = jnp.split(h @ w_qkv, [GQA_Q_HEADS * GQA_HEAD_DIM,
                                        (GQA_Q_HEADS + GQA_KV_HEADS) * GQA_HEAD_DIM], axis=-1)
        q = q.reshape(b, l, GQA_KV_HEADS, GQA_GROUP, GQA_HEAD_DIM).transpose(0, 2, 3, 1, 4)
        k = k.reshape(b, l, GQA_KV_HEADS, GQA_HEAD_DIM).transpose(0, 2, 1, 3)
        v = v.reshape(b, l, GQA_KV_HEADS, GQA_HEAD_DIM).transpose(0, 2, 1, 3)
        return rmsnorm(q, q_gain), rmsnorm(k, k_gain), v

    def merge(o):
        b, _, _, l, _ = o.shape
        return o.transpose(0, 3, 1, 2, 4).reshape(b, l, GQA_Q_HEADS * GQA_HEAD_DIM) @ w_o

    q_c, k_c, v_c = project(h_ctx)
    q_l, k_l, v_l = project(h_lat)
    q_l = apply_rope(q_l, cos, sin)
    k_l = apply_rope(k_l, cos, sin)
    k_all = jnp.concatenate([k_c, k_l], axis=2)
    v_all = jnp.concatenate([v_c, v_l], axis=2)
    o_lat = sweep_query_blocks(lambda qb: gqa_attend(qb, k_all, v_all), q_l)
    y_ctx = merge(gqa_attend(q_c, k_c, v_c)) if need_ctx else None
    return y_ctx, merge(o_lat)


def depthwise_conv3(u, w):
    return lax.conv_general_dilated(u, w[:, None, :], window_strides=(1,),
                                    padding=((CONV_WIDTH // 2, CONV_WIDTH // 2),),
                                    dimension_numbers=('NWC', 'WIO', 'NWC'),
                                    feature_group_count=u.shape[-1])


def shortconv_seq(h, w_in, w_dw, w_out):
    gate_b, gate_c, xv = jnp.split(h @ w_in, 3, axis=-1)
    return (gate_b * depthwise_conv3(gate_c * xv, w_dw)) @ w_out


def shortconv_mixer(h_ctx, h_lat, w_in, w_dw, w_out, need_ctx):
    y_ctx = shortconv_seq(h_ctx, w_in, w_dw, w_out) if need_ctx else None
    return y_ctx, shortconv_seq(h_lat, w_in, w_dw, w_out)


def diff_attend(q1, q2, k1, k2, v, lam):
    scale = DIFF_HEAD_DIM ** -0.5
    p1 = jax.nn.softmax(jnp.einsum('bhqd,bhkd->bhqk', q1, k1).astype(jnp.float32) * scale, axis=-1)
    p2 = jax.nn.softmax(jnp.einsum('bhqd,bhkd->bhqk', q2, k2).astype(jnp.float32) * scale, axis=-1)
    return jnp.einsum('bhqk,bhkd->bhqd', (p1 - lam * p2).astype(v.dtype), v)


def diff_mixer(h_ctx, h_lat, w_qkv, w_o, lam_vecs, subln_gain, lam_init, cos, sin, need_ctx):
    lv = lam_vecs.astype(jnp.float32)
    lam = jnp.exp(jnp.sum(lv[0] * lv[1])) - jnp.exp(jnp.sum(lv[2] * lv[3])) + lam_init

    def project(h):
        b, l, _ = h.shape
        q, k, v = jnp.split(h @ w_qkv, 3, axis=-1)
        q = q.reshape(b, l, DIFF_HEADS, 2, DIFF_HEAD_DIM).transpose(3, 0, 2, 1, 4)
        k = k.reshape(b, l, DIFF_HEADS, 2, DIFF_HEAD_DIM).transpose(3, 0, 2, 1, 4)
        v = v.reshape(b, l, DIFF_HEADS, 2 * DIFF_HEAD_DIM).transpose(0, 2, 1, 3)
        return q[0], q[1], k[0], k[1], v

    def merge(o):
        b, _, l, _ = o.shape
        o = rmsnorm(o, subln_gain) * (1.0 - lam_init)
        return o.transpose(0, 2, 1, 3).reshape(b, l, D_MODEL) @ w_o

    q1_c, q2_c, k1_c, k2_c, v_c = project(h_ctx)
    q1_l, q2_l, k1_l, k2_l, v_l = project(h_lat)
    q1_l, q2_l = apply_rope(q1_l, cos, sin), apply_rope(q2_l, cos, sin)
    k1_l, k2_l = apply_rope(k1_l, cos, sin), apply_rope(k2_l, cos, sin)
    k1_all = jnp.concatenate([k1_c, k1_l], axis=2)
    k2_all = jnp.concatenate([k2_c, k2_l], axis=2)
    v_all = jnp.concatenate([v_c, v_l], axis=2)
    o_lat = sweep_query_blocks(lambda a, b_: diff_attend(a, b_, k1_all, k2_all, v_all, lam), q1_l, q2_l)
    y_ctx = merge(diff_attend(q1_c, q2_c, k1_c, k2_c, v_c, lam)) if need_ctx else None
    return y_ctx, merge(o_lat)


def moe_ffn(h, w_router, b_router, w1, b1, w2, b2):
    T, D = h.shape
    logits = (h @ w_router).astype(jnp.float32) + b_router.astype(jnp.float32)
    top_val, top_idx = lax.top_k(logits, TOP_K)
    gates = jax.nn.softmax(top_val, axis=-1)
    n = T * TOP_K
    flat_e = top_idx.reshape(-1)
    flat_tok = jnp.arange(n, dtype=jnp.int32) // TOP_K
    order = jnp.argsort(flat_e)
    se, st, sg = flat_e[order], flat_tok[order], gates.reshape(-1)[order]
    counts = jnp.zeros((N_EXPERTS,), jnp.int32).at[flat_e].add(1)
    padded = (counts + EXPERT_BLOCK - 1) // EXPERT_BLOCK * EXPERT_BLOCK
    start = jnp.cumsum(counts) - counts
    pend = jnp.cumsum(padded)
    pstart = pend - padded
    dest = pstart[se] + (jnp.arange(n, dtype=jnp.int32) - start[se])
    n_blocks = -(-n // EXPERT_BLOCK) + N_EXPERTS
    n_pad = n_blocks * EXPERT_BLOCK
    row_tok = jnp.zeros((n_pad,), jnp.int32).at[dest].set(st)
    row_gate = jnp.zeros((n_pad,), jnp.float32).at[dest].set(sg)
    block_e = jnp.minimum(
        jnp.searchsorted(pend, jnp.arange(n_blocks, dtype=jnp.int32) * EXPERT_BLOCK, side='right'),
        N_EXPERTS - 1)

    def expert_block(args):
        tok, e = args
        gu = h[tok] @ w1[e] + b1[e]
        g = jnp.minimum(gu[:, :D_FF], SWIGLU_LIMIT)
        u = jnp.clip(gu[:, D_FF:], -SWIGLU_LIMIT, SWIGLU_LIMIT)
        act = g * jax.nn.sigmoid(SWIGLU_ALPHA * g) * (u + 1.0)
        return act @ w2[e] + b2[e]

    out = lax.map(expert_block, (row_tok.reshape(n_blocks, EXPERT_BLOCK), block_e))
    out = out.reshape(n_pad, D) * row_gate[:, None].astype(h.dtype)
    return jnp.zeros_like(h).at[row_tok].add(out)


def setup_inputs(seed: int = 0) -> dict:
    key = jax.random.key(seed)
    keys = list(jax.random.split(key, 32))

    def nrm(shape, scale):
        return scale * jax.random.normal(keys.pop(), shape, jnp.float32)

    def gain(shape):
        return 1.0 + nrm(shape, 0.1)

    n_a, n_b, n_c, n_d = (len(range(m, DEPTH, N_MIXERS)) for m in range(N_MIXERS))
    D = D_MODEL
    s = D ** -0.5
    return {
        "x": nrm((BATCH, SEQ, D), 1.0),
        "c": nrm((BATCH, D), 1.0),
        "ctx": nrm((BATCH, CTX_LEN, D), 1.0),
        "c_ctx": nrm((D,), 1.0),
        "mod_w": nrm((DEPTH, D, MOD_CHUNKS * D), 0.3 * s),
        "mod_b": nrm((DEPTH, MOD_CHUNKS * D), 0.02),
        "norm_g": gain((DEPTH, 4, D)),
        "pool_w": nrm((n_a, len(POOL_WINDOWS), POOL_GROUP, POOL_GROUP), POOL_GROUP ** -0.5),
        "pool_scale": gain((n_a, D)),
        "gqa_w_qkv": nrm((n_b, D, (GQA_Q_HEADS + 2 * GQA_KV_HEADS) * GQA_HEAD_DIM), s),
        "gqa_w_o": nrm((n_b, GQA_Q_HEADS * GQA_HEAD_DIM, D), (GQA_Q_HEADS * GQA_HEAD_DIM) ** -0.5),
        "gqa_q_gain": gain((n_b, GQA_HEAD_DIM)),
        "gqa_k_gain": gain((n_b, GQA_HEAD_DIM)),
        "conv_w_in": nrm((n_c, D, 3 * D), s),
        "conv_w_dw": nrm((n_c, CONV_WIDTH, D), CONV_WIDTH ** -0.5),
        "conv_w_out": nrm((n_c, D, D), s),
        "diff_w_qkv": nrm((n_d, D, 3 * D), s),
        "diff_w_o": nrm((n_d, D, D), s),
        "diff_lambda": nrm((n_d, 4, DIFF_HEAD_DIM), 0.1),
        "diff_subln_gain": gain((n_d, 2 * DIFF_HEAD_DIM)),
        "router_w": nrm((DEPTH, D, N_EXPERTS), s),
        "router_b": nrm((DEPTH, N_EXPERTS), 0.01),
        "moe_w1": nrm((DEPTH, N_EXPERTS, D, 2 * D_FF), s),
        "moe_b1": nrm((DEPTH, N_EXPERTS, 2 * D_FF), 0.02),
        "moe_w2": nrm((DEPTH, N_EXPERTS, D_FF, D), D_FF ** -0.5),
        "moe_b2": nrm((DEPTH, N_EXPERTS, D), 0.02),
    }


def reference(x, c, ctx, c_ctx, mod_w, mod_b, norm_g, pool_w, pool_scale,
              gqa_w_qkv, gqa_w_o, gqa_q_gain, gqa_k_gain,
              conv_w_in, conv_w_dw, conv_w_out,
              diff_w_qkv, diff_w_o, diff_lambda, diff_subln_gain,
              router_w, router_b, moe_w1, moe_b1, moe_w2, moe_b2):
    b, L, D = x.shape
    n_ctx = ctx.shape[1]
    rows = L // GRID_W
    row = jnp.broadcast_to(jnp.arange(rows, dtype=jnp.float32)[:, None], (rows, GRID_W)).reshape(-1)
    col = jnp.broadcast_to(jnp.arange(GRID_W, dtype=jnp.float32)[None, :], (rows, GRID_W)).reshape(-1)
    cos_b, sin_b = axial_rope_tables(row, col, GQA_HEAD_DIM)
    cos_d, sin_d = axial_rope_tables(row, col, DIFF_HEAD_DIM)
    silu_c = jax.nn.silu(c)
    silu_cc = jax.nn.silu(c_ctx)

    for i in range(DEPTH):
        m, j = i % N_MIXERS, i // N_MIXERS
        need_ctx = i < DEPTH - 1
        mod_l = jnp.split((silu_c @ mod_w[i] + mod_b[i])[:, None, :], MOD_CHUNKS, axis=-1)
        mod_c = jnp.split(silu_cc @ mod_w[i] + mod_b[i], MOD_CHUNKS, axis=-1)

        h_l = rmsnorm(x, norm_g[i, 0]) * (1.0 + mod_l[1]) + mod_l[0]
        h_c = rmsnorm(ctx, norm_g[i, 0]) * (1.0 + mod_c[1]) + mod_c[0]
        if m == 0:
            y_c, y_l = pool_mixer(h_c, h_l, pool_w[j], pool_scale[j], need_ctx)
        elif m == 1:
            y_c, y_l = gqa_mixer(h_c, h_l, gqa_w_qkv[j], gqa_w_o[j], gqa_q_gain[j], gqa_k_gain[j],
                                 cos_b, sin_b, need_ctx)
        elif m == 2:
            y_c, y_l = shortconv_mixer(h_c, h_l, conv_w_in[j], conv_w_dw[j], conv_w_out[j], need_ctx)
        else:
            lam_init = 0.8 - 0.6 * math.exp(-0.3 * i)
            y_c, y_l = diff_mixer(h_c, h_l, diff_w_qkv[j], diff_w_o[j], diff_lambda[j],
                                  diff_subln_gain[j], lam_init, cos_d, sin_d, need_ctx)
        x = x + mod_l[2] * rmsnorm(y_l, norm_g[i, 1])

        h_l = rmsnorm(x, norm_g[i, 2]) * (1.0 + mod_l[4]) + mod_l[3]
        if need_ctx:
            ctx = ctx + mod_c[2] * rmsnorm(y_c, norm_g[i, 1])
            h_c = rmsnorm(ctx, norm_g[i, 2]) * (1.0 + mod_c[4]) + mod_c[3]
            h_all = jnp.concatenate([h_c, h_l], axis=1)
        else:
            h_all = h_l
        y_all = moe_ffn(h_all.reshape(-1, D), router_w[i], router_b[i],
                        moe_w1[i], moe_b1[i], moe_w2[i], moe_b2[i]).reshape(h_all.shape)
        x = x + mod_l[5] * rmsnorm(y_all[:, h_all.shape[1] - L:], norm_g[i, 3])
        if need_ctx:
            ctx = ctx + mod_c[5] * rmsnorm(y_all[:, :n_ctx], norm_g[i, 3])
    return x
```

```python
import functools
import math

import jax
import jax.numpy as jnp
from jax import lax
from jax.experimental import pallas as pl
from jax.experimental.pallas import tpu as pltpu

RMS_EPS = 1e-6
ROPE_THETA = 10000.0
GRID_W = 64
POOL_WINDOWS = (2, 4, 8, 16)
N_EXPERTS = 32
TOP_K = 4
SWIGLU_LIMIT = 7.0
SWIGLU_ALPHA = 1.702
GQA_HEAD_DIM = 128
GQA_Q_HEADS = 8
GQA_KV_HEADS = 2
DIFF_HEAD_DIM = 64
DIFF_HEADS = 8
MOD_CHUNKS = 6

LANES = 128
SUBLANES = 8
VMEM_LIMIT_BYTES = 56 * 1024 * 1024

TM = 256
EXPERT_ROWS = 512
KV_CHUNK = 1024
HALO = SUBLANES

MXU_DTYPE = jnp.bfloat16
F32 = jnp.float32
NEG_BIG = -1e30


def _params(*sem):
    return pltpu.CompilerParams(dimension_semantics=sem, vmem_limit_bytes=VMEM_LIMIT_BYTES)


def _dot(a, b):
    return jnp.dot(a.astype(MXU_DTYPE), b.astype(MXU_DTYPE), preferred_element_type=F32)


def _rms(x, g):
    return x * lax.rsqrt(jnp.mean(x * x, axis=-1, keepdims=True) + RMS_EPS) * g


def _sigmoid(z):
    return 1.0 / (1.0 + jnp.exp(-z))


def _shift_up(a, s):
    n = a.shape[0]
    return pltpu.roll(a, (n - s) % n, 0)


def _mod_kernel(cv_ref, w_ref, b_ref, o_ref):
    a = cv_ref[...]
    o_ref[0] = _dot(a * _sigmoid(a), w_ref[0]) + b_ref[0]


def _modulation(cv, mod_w, mod_b):
    depth, d, n = mod_w.shape
    tn = n // 4
    return pl.pallas_call(
        _mod_kernel,
        grid=(depth, n // tn),
        in_specs=[pl.BlockSpec((SUBLANES, d), lambda i, k: (0, 0)),
                  pl.BlockSpec((1, d, tn), lambda i, k: (i, 0, k)),
                  pl.BlockSpec((1, 1, tn), lambda i, k: (i, 0, k))],
        out_specs=pl.BlockSpec((1, SUBLANES, tn), lambda i, k: (i, 0, k)),
        out_shape=jax.ShapeDtypeStruct((depth, SUBLANES, n), F32),
        compiler_params=_params("arbitrary", "arbitrary"),
        name="modulation",
    )(cv, mod_w, mod_b.reshape(depth, 1, n))


def _post_mixer(y, x, mod, ng, rwh_ref, rwl_ref, rb_ref, base_ref,
                x1_ref, h2_ref, mi_ref, mf_ref, cnt_ref):
    tm = x.shape[0]
    x1 = x + mod[2:3] * _rms(y, ng[1:2])
    h2 = _rms(x1, ng[2:3]) * (1.0 + mod[4:5]) + mod[3:4]
    x1_ref[0] = x1
    h2_ref[...] = h2

    h_hi = h2.astype(jnp.bfloat16)
    h_lo = (h2 - h_hi.astype(F32)).astype(jnp.bfloat16)
    rwh = rwh_ref[...]
    logits = (jnp.dot(h_hi, rwh, preferred_element_type=F32)
              + jnp.dot(h_lo, rwh, preferred_element_type=F32)
              + jnp.dot(h_hi, rwl_ref[...], preferred_element_type=F32)) + rb_ref[...]

    lane = lax.broadcasted_iota(jnp.int32, (tm, LANES), 1)
    lane_f = lane.astype(F32)
    vals, idxs, hots = [], [], []
    l = logits
    for _ in range(TOP_K):
        m = jnp.max(l, axis=-1, keepdims=True)
        idx = jnp.min(jnp.where(l == m, lane_f, float(LANES)), axis=-1, keepdims=True)
        hot = lane_f == idx
        vals.append(m)
        idxs.append(idx)
        hots.append(hot)
        l = jnp.where(hot, -3e38, l)
    exps = [jnp.exp(v - vals[0]) for v in vals]
    den = exps[0] + exps[1] + exps[2] + exps[3]

    @pl.when((pl.program_id(0) == 0) & (pl.program_id(1) == 0))
    def _():
        base_ref[...] = jnp.zeros_like(base_ref)

    hot_all = (hots[0].astype(F32) + hots[1].astype(F32) + hots[2].astype(F32) + hots[3].astype(F32))
    rows = lax.broadcasted_iota(jnp.int32, (tm, tm), 0)
    cols = lax.broadcasted_iota(jnp.int32, (tm, tm), 1)
    lower = (rows > cols).astype(jnp.bfloat16)
    before = jnp.dot(lower, hot_all.astype(jnp.bfloat16), preferred_element_type=F32) + base_ref[...]
    mi = jnp.zeros((tm, LANES), F32)
    mf = jnp.zeros((tm, LANES), F32)
    for k in range(TOP_K):
        rank = jnp.sum(jnp.where(hots[k], before, 0.0), axis=-1, keepdims=True)
        mi = mi + jnp.where(lane == k, idxs[k], 0.0) + jnp.where(lane == TOP_K + k, rank, 0.0)
        mf = mf + jnp.where(lane == k, exps[k] / den, 0.0)
    mi_ref[...] = mi.astype(jnp.int32)
    mf_ref[...] = mf
    base_ref[...] = base_ref[...] + jnp.sum(hot_all, axis=0, keepdims=True)
    cnt_ref[...] = base_ref[...]


def _mixer_call(body, x, modt, ng, router, extra_args, extra_specs, name):
    b, s, d = x.shape
    nt = s // TM
    rwh, rwl, rb = router
    n_extra = len(extra_args)

    def kern(*refs):
        x_ref, mod_ref, ng_ref, rwh_ref, rwl_ref, rb_ref = refs[:6]
        extra = refs[6:6 + n_extra]
        x1_ref, h2_ref, mi_ref, mf_ref, cnt_ref, base_ref = refs[6 + n_extra:]
        xt = x_ref[0]
        mod = mod_ref[0]
        ngv = ng_ref[...]
        y = body(xt, mod, ngv, *extra)
        _post_mixer(y, xt, mod, ngv, rwh_ref, rwl_ref, rb_ref, base_ref,
                    x1_ref, h2_ref, mi_ref, mf_ref, cnt_ref)

    in_specs = [
        pl.BlockSpec((1, TM, d), lambda bi, j: (bi, j, 0)),
        pl.BlockSpec((1, SUBLANES, d), lambda bi, j: (bi * 2 + jnp.minimum(j, 1), 0, 0)),
        pl.BlockSpec((4, d), lambda bi, j: (0, 0)),
        pl.BlockSpec((d, LANES), lambda bi, j: (0, 0)),
        pl.BlockSpec((d, LANES), lambda bi, j: (0, 0)),
        pl.BlockSpec((1, LANES), lambda bi, j: (0, 0)),
    ] + list(extra_specs)
    out_specs = [
        pl.BlockSpec((1, TM, d), lambda bi, j: (bi, j, 0)),
        pl.BlockSpec((TM, d), lambda bi, j: (bi * nt + j, 0)),
        pl.BlockSpec((TM, LANES), lambda bi, j: (bi * nt + j, 0)),
        pl.BlockSpec((TM, LANES), lambda bi, j: (bi * nt + j, 0)),
        pl.BlockSpec((1, LANES), lambda bi, j: (0, 0)),
    ]
    out_shape = [
        jax.ShapeDtypeStruct((b, s, d), F32),
        jax.ShapeDtypeStruct((b * s, d), F32),
        jax.ShapeDtypeStruct((b * s, LANES), jnp.int32),
        jax.ShapeDtypeStruct((b * s, LANES), F32),
        jax.ShapeDtypeStruct((1, LANES), F32),
    ]
    return pl.pallas_call(
        kern, grid=(b, nt), in_specs=in_specs, out_specs=out_specs, out_shape=out_shape,
        scratch_shapes=[pltpu.VMEM((1, LANES), F32)],
        compiler_params=_params("arbitrary", "arbitrary"), name=name,
    )(x, modt, ng, rwh, rwl, rb, *extra_args)


def _halo_specs(s, d):
    last = s // HALO - 1
    per_tile = TM // HALO
    prev = pl.BlockSpec((1, HALO, d), lambda bi, j: (bi, jnp.maximum(j * per_tile - 1, 0), 0))
    nxt = pl.BlockSpec((1, HALO, d), lambda bi, j: (bi, jnp.minimum((j + 1) * per_tile, last), 0))
    return prev, nxt


def _halo_valid():
    j = pl.program_id(1)
    nt = pl.num_programs(1)
    return j >= 2, (j >= 1) & (j <= nt - 2)


def _pool_body(ctx_len, lat_len, xt, mod, ng, xp_ref, xn_ref, pw_ref, ps_ref):
    tm, d = xt.shape
    group = d // len(POOL_WINDOWS)
    prev_ok, next_ok = _halo_valid()
    pre = lambda rows: _rms(rows, ng[0:1]) * (1.0 + mod[1:2]) + mod[0:1]
    hc = pre(xt)
    hp = jnp.where(prev_ok, pre(xp_ref[0]), 0.0)
    hn = jnp.where(next_ok, pre(xn_ref[0]), 0.0)
    ext = jnp.concatenate([hp, hc, hn], axis=0)

    j = pl.program_id(1)
    pos = lax.broadcasted_iota(jnp.int32, (tm, 1), 0) + jnp.where(j == 0, 0, (j - 1) * tm)
    seq_len = jnp.where(j == 0, ctx_len, lat_len)
    outs = []
    for g, w in enumerate(POOL_WINDOWS):
        e = ext[:, g * group:(g + 1) * group]
        run, span = e, 1
        while span < w:
            run = run + _shift_up(run, span)
            span *= 2
        win = _shift_up(run, HALO - w // 2)[:tm] if w // 2 != HALO else run[:tm]
        lo = jnp.maximum(pos - w // 2, 0)
        hi = jnp.minimum(pos + (w - w // 2), seq_len)
        diff = win / (hi - lo).astype(F32) - hc[:, g * group:(g + 1) * group]
        outs.append(_dot(diff, pw_ref[g]))
    return jnp.concatenate(outs, axis=1) * ps_ref[...]


def _pool_layer(x, modt, ng, router, pool_w, pool_scale, ctx_len):
    b, s, d = x.shape
    prev, nxt = _halo_specs(s, d)
    g = len(POOL_WINDOWS)
    specs = [prev, nxt,
             pl.BlockSpec((g, d // g, d // g), lambda bi, j: (0, 0, 0)),
             pl.BlockSpec((1, d), lambda bi, j: (0, 0))]
    body = functools.partial(_pool_body, ctx_len, s - ctx_len)
    return _mixer_call(body, x, modt, ng, router,
                       (x, x, pool_w.astype(MXU_DTYPE), pool_scale.reshape(1, d)), specs, "pool_mixer")


def _conv_body(xt, mod, ng, xp_ref, xn_ref, win_ref, wdw_ref, wout_ref):
    tm, d = xt.shape
    prev_ok, next_ok = _halo_valid()
    rows = jnp.concatenate([xp_ref[0], xt, xn_ref[0]], axis=0)
    h = _rms(rows, ng[0:1]) * (1.0 + mod[1:2]) + mod[0:1]
    proj = _dot(h, win_ref[...])
    gate_b = proj[HALO:HALO + tm, :d]
    u = proj[:, d:2 * d] * proj[:, 2 * d:]
    r = lax.broadcasted_iota(jnp.int32, (tm + 2 * HALO, 1), 0)
    keep = ((r >= HALO) | prev_ok) & ((r < HALO + tm) | next_ok)
    u = jnp.where(keep, u, 0.0)
    wdw = wdw_ref[...]
    conv = (wdw[0:1] * _shift_up(u, HALO - 1)[:tm] + wdw[1:2] * u[HALO:HALO + tm]
            + wdw[2:3] * _shift_up(u, HALO + 1)[:tm])
    return _dot(gate_b * conv, wout_ref[...])


def _conv_layer(x, modt, ng, router, w_in, w_dw, w_out):
    b, s, d = x.shape
    prev, nxt = _halo_specs(s, d)
    specs = [prev, nxt,
             pl.BlockSpec((d, 3 * d), lambda bi, j: (0, 0)),
             pl.BlockSpec((SUBLANES, d), lambda bi, j: (0, 0)),
             pl.BlockSpec((d, d), lambda bi, j: (0, 0))]
    w_dw8 = jnp.zeros((SUBLANES, d), F32).at[:w_dw.shape[0]].set(w_dw)
    return _mixer_call(_conv_body, x, modt, ng, router,
                       (x, x, w_in.astype(MXU_DTYPE), w_dw8, w_out.astype(MXU_DTYPE)), specs, "conv_mixer")


def _gqa_qkv_kernel(x_ref, mod_ref, ng_ref, w_ref, qg_ref, kg_ref, cos_ref, sin_ref,
                    q_ref, k_ref, v_ref):
    mod = mod_ref[0]
    h = _rms(x_ref[0], ng_ref[0:1]) * (1.0 + mod[1:2]) + mod[0:1]
    qkv = _dot(h, w_ref[...])
    cos, sin = cos_ref[...], sin_ref[...]
    hd = GQA_HEAD_DIM

    def rope(t):
        return t * cos + pltpu.roll(t, hd // 2, 1) * sin

    scale = hd ** -0.5
    for i in range(GQA_Q_HEADS):
        q = rope(_rms(qkv[:, i * hd:(i + 1) * hd], qg_ref[...]))
        q_ref[0, i] = (q * scale).astype(q_ref.dtype)
    for i in range(GQA_KV_HEADS):
        o = (GQA_Q_HEADS + i) * hd
        k_ref[0, i] = rope(_rms(qkv[:, o:o + hd], kg_ref[...])).astype(k_ref.dtype)
        o = (GQA_Q_HEADS + GQA_KV_HEADS + i) * hd
        v_ref[0, i] = qkv[:, o:o + hd].astype(v_ref.dtype)


def _gqa_qkv(x, modt, ng, w_qkv, q_gain, k_gain, cos, sin):
    b, s, d = x.shape
    hd = GQA_HEAD_DIM
    n = w_qkv.shape[1]
    return pl.pallas_call(
        _gqa_qkv_kernel, grid=(b, s // TM),
        in_specs=[pl.BlockSpec((1, TM, d), lambda bi, j: (bi, j, 0)),
                  pl.BlockSpec((1, SUBLANES, d), lambda bi, j: (bi * 2 + jnp.minimum(j, 1), 0, 0)),
                  pl.BlockSpec((4, d), lambda bi, j: (0, 0)),
                  pl.BlockSpec((d, n), lambda bi, j: (0, 0)),
                  pl.BlockSpec((1, hd), lambda bi, j: (0, 0)),
                  pl.BlockSpec((1, hd), lambda bi, j: (0, 0)),
                  pl.BlockSpec((TM, hd), lambda bi, j: (j, 0)),
                  pl.BlockSpec((TM, hd), lambda bi, j: (j, 0))],
        out_specs=[pl.BlockSpec((1, GQA_Q_HEADS, TM, hd), lambda bi, j: (bi, 0, j, 0)),
                   pl.BlockSpec((1, GQA_KV_HEADS, TM, hd), lambda bi, j: (bi, 0, j, 0)),
                   pl.BlockSpec((1, GQA_KV_HEADS, TM, hd), lambda bi, j: (bi, 0, j, 0))],
        out_shape=[jax.ShapeDtypeStruct((b, GQA_Q_HEADS, s, hd), MXU_DTYPE),
                   jax.ShapeDtypeStruct((b, GQA_KV_HEADS, s, hd), MXU_DTYPE),
                   jax.ShapeDtypeStruct((b, GQA_KV_HEADS, s, hd), MXU_DTYPE)],
        compiler_params=_params("arbitrary", "arbitrary"), name="gqa_qkv",
    )(x, modt, ng, w_qkv.astype(MXU_DTYPE), q_gain.reshape(1, hd), k_gain.reshape(1, hd), cos, sin)


def _diff_qkv_kernel(x_ref, mod_ref, ng_ref, w_ref, cos_ref, sin_ref, q_ref, k_ref, v_ref):
    mod = mod_ref[0]
    d = x_ref.shape[2]
    h = _rms(x_ref[0], ng_ref[0:1]) * (1.0 + mod[1:2]) + mod[0:1]
    qkv = _dot(h, w_ref[...])
    cos, sin = cos_ref[...], sin_ref[...]
    hw = 2 * DIFF_HEAD_DIM
    lane = lax.broadcasted_iota(jnp.int32, (x_ref.shape[1], hw), 1)
    quarter = DIFF_HEAD_DIM // 2
    take_up = (lane // quarter) % 2 == 0

    def rope(t):
        rot = jnp.where(take_up, pltpu.roll(t, hw - quarter, 1), pltpu.roll(t, quarter, 1))
        return t * cos + rot * sin

    scale = DIFF_HEAD_DIM ** -0.5
    first = lane < DIFF_HEAD_DIM
    for i in range(DIFF_HEADS):
        q = rope(qkv[:, i * hw:(i + 1) * hw]) * scale
        q_ref[0, 2 * i] = jnp.where(first, q, 0.0).astype(q_ref.dtype)
        q_ref[0, 2 * i + 1] = jnp.where(first, 0.0, q).astype(q_ref.dtype)
        k_ref[0, i] = rope(qkv[:, d + i * hw:d + (i + 1) * hw]).astype(k_ref.dtype)
        v_ref[0, i] = qkv[:, 2 * d + i * hw:2 * d + (i + 1) * hw].astype(v_ref.dtype)


def _diff_qkv(x, modt, ng, w_qkv, cos, sin):
    b, s, d = x.shape
    hw = 2 * DIFF_HEAD_DIM
    n = w_qkv.shape[1]
    return pl.pallas_call(
        _diff_qkv_kernel, grid=(b, s // TM),
        in_specs=[pl.BlockSpec((1, TM, d), lambda bi, j: (bi, j, 0)),
                  pl.BlockSpec((1, SUBLANES, d), lambda bi, j: (bi * 2 + jnp.minimum(j, 1), 0, 0)),
                  pl.BlockSpec((4, d), lambda bi, j: (0, 0)),
                  pl.BlockSpec((d, n), lambda bi, j: (0, 0)),
                  pl.BlockSpec((TM, hw), lambda bi, j: (j, 0)),
                  pl.BlockSpec((TM, hw), lambda bi, j: (j, 0))],
        out_specs=[pl.BlockSpec((1, 2 * DIFF_HEADS, TM, hw), lambda bi, j: (bi, 0, j, 0)),
                   pl.BlockSpec((1, DIFF_HEADS, TM, hw), lambda bi, j: (bi, 0, j, 0)),
                   pl.BlockSpec((1, DIFF_HEADS, TM, hw), lambda bi, j: (bi, 0, j, 0))],
        out_shape=[jax.ShapeDtypeStruct((b, 2 * DIFF_HEADS, s, hw), MXU_DTYPE),
                   jax.ShapeDtypeStruct((b, DIFF_HEADS, s, hw), MXU_DTYPE),
                   jax.ShapeDtypeStruct((b, DIFF_HEADS, s, hw), MXU_DTYPE)],
        compiler_params=_params("arbitrary", "arbitrary"), name="diff_qkv",
    )(x, modt, ng, w_qkv.astype(MXU_DTYPE), cos, sin)


def _flash_kernel(*refs, group, ctx_len, kv_chunk, n_chunks, lam_init):
    if lam_init is None:
        q_ref, k_ref, v_ref, o_ref = refs
    else:
        q_ref, k_ref, v_ref, lam_ref, sg_ref, o_ref = refs
    tq, hd = q_ref.shape[2], q_ref.shape[3]
    j = pl.program_id(2)
    q = q_ref[0].reshape(group * tq, hd)

    def step(carry, start, size):
        m, l, acc = carry
        k = k_ref[0, 0, pl.ds(start, size), :]
        v = v_ref[0, 0, pl.ds(start, size), :]
        s = lax.dot_general(q, k, (((1,), (1,)), ((), ())), preferred_element_type=F32)
        m_new = jnp.maximum(m, jnp.max(s, axis=-1, keepdims=True))
        alpha = jnp.exp(m - m_new)
        p = jnp.exp(s - m_new)
        l = alpha * l + jnp.sum(p, axis=-1, keepdims=True)
        acc = alpha * acc + jnp.dot(p.astype(v.dtype), v, preferred_element_type=F32)
        return m_new, l, acc

    rows = group * tq
    carry = (jnp.full((rows, 1), NEG_BIG, F32), jnp.zeros((rows, 1), F32), jnp.zeros((rows, hd), F32))
    carry = step(carry, 0, ctx_len)
    n = jnp.where(j > 0, n_chunks, 0)
    carry = lax.fori_loop(
        0, n, lambda i, c: step(c, pl.multiple_of(ctx_len + i * kv_chunk, TM), kv_chunk), carry)
    _, l, acc = carry
    o = acc / l
    if lam_init is None:
        for g in range(group):
            o_ref[0, :, g * hd:(g + 1) * hd] = o[g * tq:(g + 1) * tq].astype(o_ref.dtype)
    else:
        lv = lam_ref[...]
        lam = (jnp.exp(jnp.sum(lv[0:1] * lv[1:2], axis=-1, keepdims=True))
               - jnp.exp(jnp.sum(lv[2:3] * lv[3:4], axis=-1, keepdims=True)) + lam_init)
        od = o[:tq] - lam * o[tq:]
        o_ref[0] = (_rms(od, sg_ref[...]) * (1.0 - lam_init)).astype(o_ref.dtype)


def _flash(q, k, v, group, ctx_len, lam=None, subln=None, lam_init=None):
    b, hq, s, hd = q.shape
    hkv = k.shape[1]
    lat = s - ctx_len
    kv_chunk = min(KV_CHUNK, lat)
    kern = functools.partial(_flash_kernel, group=group, ctx_len=ctx_len, kv_chunk=kv_chunk,
                             n_chunks=lat // kv_chunk, lam_init=lam_init)
    in_specs = [pl.BlockSpec((1, group, TM, hd), lambda bi, h, j: (bi, h, j, 0)),
                pl.BlockSpec((1, 1, s, hd), lambda bi, h, j: (bi, h, 0, 0)),
                pl.BlockSpec((1, 1, s, hd), lambda bi, h, j: (bi, h, 0, 0))]
    args = [q, k, v]
    if lam_init is None:
        out_w = group * hd
    else:
        out_w = hd
        in_specs += [pl.BlockSpec(lam.shape, lambda bi, h, j: (0, 0)),
                     pl.BlockSpec((1, hd), lambda bi, h, j: (0, 0))]
        args += [lam, subln.reshape(1, hd)]
    return pl.pallas_call(
        kern, grid=(b, hkv, s // TM), in_specs=in_specs,
        out_specs=pl.BlockSpec((1, TM, out_w), lambda bi, h, j: (bi, j, h)),
        out_shape=jax.ShapeDtypeStruct((b, s, hkv * out_w), MXU_DTYPE),
        compiler_params=_params("arbitrary", "arbitrary", "arbitrary"), name="flash_attention",
    )(*args)


def _attn_out_body(xt, mod, ng, o_ref, wo_ref):
    return _dot(o_ref[0], wo_ref[...])


def _attn_out_layer(x, modt, ng, router, o, w_o):
    b, s, d = x.shape
    specs = [pl.BlockSpec((1, TM, d), lambda bi, j: (bi, j, 0)),
             pl.BlockSpec((d, d), lambda bi, j: (0, 0))]
    return _mixer_call(_attn_out_body, x, modt, ng, router, (o, w_o.astype(MXU_DTYPE)), specs, "attn_out")


def _rope_tables(ctx_len, lat_len, dim, reps):
    n = dim // 4
    pos = jnp.arange(lat_len, dtype=jnp.int32)
    row = (pos // GRID_W).astype(F32)
    col = (pos % GRID_W).astype(F32)
    freqs = ROPE_THETA ** (-jnp.arange(n, dtype=F32) / n)
    ang = jnp.concatenate([row[:, None] * freqs, col[:, None] * freqs], axis=-1)
    ang = jnp.concatenate([jnp.zeros((ctx_len, dim // 2), F32), ang], axis=0)
    cos, sin = jnp.cos(ang), jnp.sin(ang)
    return jnp.tile(cos, (1, 2 * reps)), jnp.tile(jnp.concatenate([-sin, sin], axis=-1), (1, reps))


def _dispatch_kernel(dest_ref, h_hbm, zeros_hbm, hs_hbm, sem):
    del zeros_hbm
    t0 = pl.program_id(0) * TM

    def row_copy(src_row, dst_row):
        return pltpu.make_async_copy(h_hbm.at[pl.ds(src_row, 1)], hs_hbm.at[pl.ds(dst_row, 1)], sem)

    def issue(r, c):
        for k in range(TOP_K):
            row_copy(t0 + r, dest_ref[0, 0, r * TOP_K + k]).start()
        return c

    def drain(r, c):
        for k in range(TOP_K):
            row_copy(t0 + r, dest_ref[0, 0, r * TOP_K + k]).wait()
        return c

    lax.fori_loop(0, TM, issue, 0)
    lax.fori_loop(0, TM, drain, 0)


def _dispatch(h2, dest3, n_pad):
    t, d = h2.shape
    return pl.pallas_call(
        _dispatch_kernel, grid=(t // TM,),
        in_specs=[pl.BlockSpec((1, 1, TM * TOP_K), lambda i: (i, 0, 0), memory_space=pltpu.SMEM),
                  pl.BlockSpec(memory_space=pl.ANY),
                  pl.BlockSpec(memory_space=pl.ANY)],
        out_specs=pl.BlockSpec(memory_space=pl.ANY),
        out_shape=jax.ShapeDtypeStruct((n_pad, d), F32),
        scratch_shapes=[pltpu.SemaphoreType.DMA],
        input_output_aliases={2: 0},
        compiler_params=pltpu.CompilerParams(dimension_semantics=("arbitrary",), has_side_effects=True),
        name="moe_dispatch",
    )(dest3, h2, jnp.zeros((n_pad, d), F32))


def _expert_kernel(be_ref, nused_ref, hs_ref, w1_ref, b1_ref, w2_ref, b2_ref, o_ref):
    del be_ref
    i = pl.program_id(0)
    f = w2_ref.shape[1]

    @pl.when(i < nused_ref[0])
    def _():
        gu = _dot(hs_ref[...], w1_ref[0]) + b1_ref[0]
        g = jnp.minimum(gu[:, :f], SWIGLU_LIMIT)
        u = jnp.clip(gu[:, f:], -SWIGLU_LIMIT, SWIGLU_LIMIT)
        act = g * _sigmoid(SWIGLU_ALPHA * g) * (u + 1.0)
        o_ref[...] = _dot(act, w2_ref[0]) + b2_ref[0]

    @pl.when(i >= nused_ref[0])
    def _():
        o_ref[...] = jnp.zeros_like(o_ref)


def _experts(hs, block_e, n_used, w1, b1, w2, b2):
    n_pad, d = hs.shape
    e, _, f2 = w1.shape
    f = w2.shape[1]
    tb = EXPERT_ROWS
    grid_spec = pltpu.PrefetchScalarGridSpec(
        num_scalar_prefetch=2, grid=(n_pad // tb,),
        in_specs=[pl.BlockSpec((tb, d), lambda i, be, nu: (i, 0)),
                  pl.BlockSpec((1, d, f2), lambda i, be, nu: (be[i], 0, 0)),
                  pl.BlockSpec((1, 1, f2), lambda i, be, nu: (be[i], 0, 0)),
                  pl.BlockSpec((1, f, d), lambda i, be, nu: (be[i], 0, 0)),
                  pl.BlockSpec((1, 1, d), lambda i, be, nu: (be[i], 0, 0))],
        out_specs=pl.BlockSpec((tb, d), lambda i, be, nu: (i, 0)))
    return pl.pallas_call(
        _expert_kernel, grid_spec=grid_spec,
        out_shape=jax.ShapeDtypeStruct((n_pad, d), F32),
        compiler_params=_params("arbitrary"), name="moe_experts",
    )(block_e, n_used, hs, w1, b1.reshape(e, 1, f2), w2, b2.reshape(e, 1, d))


def _combine_kernel(dest_ref, ys_hbm, mf_ref, x1_ref, mod_ref, ng_ref, x2_ref, buf, sem):
    def row_copy(src_row, k, r):
        return pltpu.make_async_copy(ys_hbm.at[pl.ds(src_row, 1)], buf.at[k, pl.ds(r, 1)], sem)

    def issue(r, c):
        for k in range(TOP_K):
            row_copy(dest_ref[0, 0, r * TOP_K + k], k, r).start()
        return c

    def drain(r, c):
        for k in range(TOP_K):
            row_copy(dest_ref[0, 0, r * TOP_K + k], k, r).wait()
        return c

    lax.fori_loop(0, TM, issue, 0)
    lax.fori_loop(0, TM, drain, 0)
    gates = mf_ref[...]
    y = gates[:, 0:1] * buf[0]
    for k in range(1, TOP_K):
        y = y + gates[:, k:k + 1] * buf[k]
    mod = mod_ref[0]
    x2_ref[0] = x1_ref[0] + mod[5:6] * _rms(y, ng_ref[3:4])


def _combine(ys, dest3, mf, x1, modt, ng):
    b, s, d = x1.shape
    nt = s // TM
    return pl.pallas_call(
        _combine_kernel, grid=(b, nt),
        in_specs=[pl.BlockSpec((1, 1, TM * TOP_K), lambda bi, j: (bi * nt + j, 0, 0), memory_space=pltpu.SMEM),
                  pl.BlockSpec(memory_space=pl.ANY),
                  pl.BlockSpec((TM, LANES), lambda bi, j: (bi * nt + j, 0)),
                  pl.BlockSpec((1, TM, d), lambda bi, j: (bi, j, 0)),
                  pl.BlockSpec((1, SUBLANES, d), lambda bi, j: (bi * 2 + jnp.minimum(j, 1), 0, 0)),
                  pl.BlockSpec((4, d), lambda bi, j: (0, 0))],
        out_specs=pl.BlockSpec((1, TM, d), lambda bi, j: (bi, j, 0)),
        out_shape=jax.ShapeDtypeStruct((b, s, d), F32),
        scratch_shapes=[pltpu.VMEM((TOP_K, TM, d), F32), pltpu.SemaphoreType.DMA],
        compiler_params=_params("arbitrary", "arbitrary"), name="moe_combine",
    )(dest3, ys, mf, x1, modt, ng)


def _moe(x1, h2, mi, mf, cnt, modt, ng, w1, b1, w2, b2):
    t = h2.shape[0]
    tb = EXPERT_ROWS
    counts = cnt[0, :N_EXPERTS].astype(jnp.int32)
    padded = (counts + tb - 1) // tb * tb
    pend = jnp.cumsum(padded)
    pstart = pend - padded
    dest = pstart[mi[:, :TOP_K]] + mi[:, TOP_K:2 * TOP_K]
    n_blocks = -(-(t * TOP_K) // tb) + N_EXPERTS
    block_e = jnp.minimum(
        jnp.searchsorted(pend, jnp.arange(n_blocks, dtype=jnp.int32) * tb, side='right'),
        N_EXPERTS - 1).astype(jnp.int32)
    n_used = (pend[-1:] // tb).astype(jnp.int32)
    dest3 = dest.reshape(t // TM, 1, TM * TOP_K)
    hs = _dispatch(h2, dest3, n_blocks * tb)
    ys = _experts(hs, block_e, n_used, w1, b1, w2, b2)
    return _combine(ys, dest3, mf, x1, modt, ng)


def kernel(x, c, ctx, c_ctx, mod_w, mod_b, norm_g, pool_w, pool_scale, gqa_w_qkv, gqa_w_o, gqa_q_gain, gqa_k_gain, conv_w_in, conv_w_dw, conv_w_out, diff_w_qkv, diff_w_o, diff_lambda, diff_subln_gain, router_w, router_b, moe_w1, moe_b1, moe_w2, moe_b2):
    b, lat_len, d = x.shape
    ctx_len = ctx.shape[1]
    depth = mod_w.shape[0]
    n_mixers = 4
    assert ctx_len == TM and lat_len % TM == 0 and b + 1 <= SUBLANES and d % LANES == 0

    xa = jnp.concatenate([ctx, x], axis=1)

    cv = jnp.zeros((SUBLANES, d), F32).at[:b].set(c).at[b].set(c_ctx)
    mods = _modulation(cv, mod_w, mod_b).reshape(depth, SUBLANES, MOD_CHUNKS, d)
    modt_all = jnp.stack([jnp.broadcast_to(mods[:, b][:, None], (depth, b, MOD_CHUNKS, d)), mods[:, :b]], axis=2)
    modt_all = jnp.pad(modt_all, ((0, 0), (0, 0), (0, 0), (0, SUBLANES - MOD_CHUNKS), (0, 0)))
    modt_all = modt_all.reshape(depth, b * 2, SUBLANES, d)

    cos_b, sin_b = _rope_tables(ctx_len, lat_len, GQA_HEAD_DIM, 1)
    cos_d, sin_d = _rope_tables(ctx_len, lat_len, DIFF_HEAD_DIM, 2)

    for i in range(depth):
        m, jj = i % n_mixers, i // n_mixers
        modt, ng = modt_all[i], norm_g[i]
        rw = jnp.zeros((d, LANES), F32).at[:, :N_EXPERTS].set(router_w[i])
        rwh = rw.astype(jnp.bfloat16)
        rwl = (rw - rwh.astype(F32)).astype(jnp.bfloat16)
        rb = jnp.full((1, LANES), NEG_BIG, F32).at[0, :N_EXPERTS].set(router_b[i])
        router = (rwh, rwl, rb)
        if m == 0:
            res = _pool_layer(xa, modt, ng, router, pool_w[jj], pool_scale[jj], ctx_len)
        elif m == 1:
            q, k, v = _gqa_qkv(xa, modt, ng, gqa_w_qkv[jj], gqa_q_gain[jj], gqa_k_gain[jj], cos_b, sin_b)
            o = _flash(q, k, v, GQA_Q_HEADS // GQA_KV_HEADS, ctx_len)
            res = _attn_out_layer(xa, modt, ng, router, o, gqa_w_o[jj])
        elif m == 2:
            res = _conv_layer(xa, modt, ng, router, conv_w_in[jj], conv_w_dw[jj], conv_w_out[jj])
        else:
            lam_init = 0.8 - 0.6 * math.exp(-0.3 * i)
            q, k, v = _diff_qkv(xa, modt, ng, diff_w_qkv[jj], cos_d, sin_d)
            o = _flash(q, k, v, 2, ctx_len, lam=diff_lambda[jj], subln=diff_subln_gain[jj], lam_init=lam_init)
            res = _attn_out_layer(xa, modt, ng, router, o, diff_w_o[jj])
        x1, h2, mi, mf, cnt = res
        xa = _moe(x1, h2, mi, mf, cnt, modt, ng,
                  moe_w1[i].astype(MXU_DTYPE), moe_b1[i], moe_w2[i].astype(MXU_DTYPE), moe_b2[i])
    return xa[:, ctx_len:]
```

```python
import functools
import math

import jax
import jax.numpy as jnp
from jax import lax
from jax.experimental import pallas as pl
from jax.experimental.pallas import tpu as pltpu

RMS_EPS = 1e-6
ROPE_THETA = 10000.0
GRID_W = 64
POOL_WINDOWS = (2, 4, 8, 16)
N_EXPERTS = 32
TOP_K = 4
SWIGLU_LIMIT = 7.0
SWIGLU_ALPHA = 1.702
GQA_HEAD_DIM = 128
GQA_Q_HEADS = 8
GQA_KV_HEADS = 2
DIFF_HEAD_DIM = 64
DIFF_HEADS = 8
MOD_CHUNKS = 6

LANES = 128
SUBLANES = 8
VMEM_LIMIT_BYTES = 56 * 1024 * 1024

TM = 256
EXPERT_ROWS = 512
KV_CHUNK = 1024
HALO = SUBLANES

MXU_DTYPE = jnp.bfloat16
F32 = jnp.float32
NEG_BIG = -1e30


def _params(*sem):
    return pltpu.CompilerParams(dimension_semantics=sem, vmem_limit_bytes=VMEM_LIMIT_BYTES)


def _dot(a, b):
    return jnp.dot(a.astype(MXU_DTYPE), b.astype(MXU_DTYPE), preferred_element_type=F32)


def _rms(x, g):
    return x * lax.rsqrt(jnp.mean(x * x, axis=-1, keepdims=True) + RMS_EPS) * g


def _sigmoid(z):
    return 1.0 / (1.0 + jnp.exp(-z))


def _shift_up(a, s):
    n = a.shape[0]
    return pltpu.roll(a, (n - s) % n, 0)


def _mod_kernel(cv_ref, w_ref, b_ref, o_ref):
    a = cv_ref[...]
    o_ref[0] = _dot(a * _sigmoid(a), w_ref[0]) + b_ref[0]


def _modulation(cv, mod_w, mod_b):
    depth, d, n = mod_w.shape
    tn = n // 4
    return pl.pallas_call(
        _mod_kernel,
        grid=(depth, n // tn),
        in_specs=[pl.BlockSpec((SUBLANES, d), lambda i, k: (0, 0)),
                  pl.BlockSpec((1, d, tn), lambda i, k: (i, 0, k)),
                  pl.BlockSpec((1, 1, tn), lambda i, k: (i, 0, k))],
        out_specs=pl.BlockSpec((1, SUBLANES, tn), lambda i, k: (i, 0, k)),
        out_shape=jax.ShapeDtypeStruct((depth, SUBLANES, n), F32),
        compiler_params=_params("arbitrary", "arbitrary"),
        name="modulation",
    )(cv, mod_w, mod_b.reshape(depth, 1, n))


def _post_mixer(y, x, mod, ng, rwh_ref, rwl_ref, rb_ref, base_ref,
                x1_ref, h2_ref, mi_ref, mf_ref, cnt_ref):
    tm = x.shape[0]
    x1 = x + mod[2:3] * _rms(y, ng[1:2])
    h2 = _rms(x1, ng[2:3]) * (1.0 + mod[4:5]) + mod[3:4]
    x1_ref[0] = x1
    h2_ref[...] = h2

    h_hi = h2.astype(jnp.bfloat16)
    h_lo = (h2 - h_hi.astype(F32)).astype(jnp.bfloat16)
    rwh = rwh_ref[...]
    logits = (jnp.dot(h_hi, rwh, preferred_element_type=F32)
              + jnp.dot(h_lo, rwh, preferred_element_type=F32)
              + jnp.dot(h_hi, rwl_ref[...], preferred_element_type=F32)) + rb_ref[...]

    lane = lax.broadcasted_iota(jnp.int32, (tm, LANES), 1)
    lane_f = lane.astype(F32)
    vals, idxs, hots = [], [], []
    l = logits
    for _ in range(TOP_K):
        m = jnp.max(l, axis=-1, keepdims=True)
        idx = jnp.min(jnp.where(l == m, lane_f, float(LANES)), axis=-1, keepdims=True)
        hot = lane_f == idx
        vals.append(m)
        idxs.append(idx)
        hots.append(hot)
        l = jnp.where(hot, -3e38, l)
    exps = [jnp.exp(v - vals[0]) for v in vals]
    den = exps[0] + exps[1] + exps[2] + exps[3]

    @pl.when((pl.program_id(0) == 0) & (pl.program_id(1) == 0))
    def _():
        base_ref[...] = jnp.zeros_like(base_ref)

    hot_all = (hots[0].astype(F32) + hots[1].astype(F32) + hots[2].astype(F32) + hots[3].astype(F32))
    rows = lax.broadcasted_iota(jnp.int32, (tm, tm), 0)
    cols = lax.broadcasted_iota(jnp.int32, (tm, tm), 1)
    lower = (rows > cols).astype(jnp.bfloat16)
    before = jnp.dot(lower, hot_all.astype(jnp.bfloat16), preferred_element_type=F32) + base_ref[...]
    mi = jnp.zeros((tm, LANES), F32)
    mf = jnp.zeros((tm, LANES), F32)
    for k in range(TOP_K):
        rank = jnp.sum(jnp.where(hots[k], before, 0.0), axis=-1, keepdims=True)
        mi = mi + jnp.where(lane == k, idxs[k], 0.0) + jnp.where(lane == TOP_K + k, rank, 0.0)
        mf = mf + jnp.where(lane == k, exps[k] / den, 0.0)
    mi_ref[...] = mi.astype(jnp.int32)
    mf_ref[...] = mf
    base_ref[...] = base_ref[...] + jnp.sum(hot_all, axis=0, keepdims=True)
    cnt_ref[...] = base_ref[...]


def _mixer_call(body, x, modt, ng, router, extra_args, extra_specs, name):
    b, s, d = x.shape
    nt = s // TM
    rwh, rwl, rb = router
    n_extra = len(extra_args)

    def kern(*refs):
        x_ref, mod_ref, ng_ref, rwh_ref, rwl_ref, rb_ref = refs[:6]
        extra = refs[6:6 + n_extra]
        x1_ref, h2_ref, mi_ref, mf_ref, cnt_ref, base_ref = refs[6 + n_extra:]
        xt = x_ref[0]
        mod = mod_ref[0]
        ngv = ng_ref[...]
        y = body(xt, mod, ngv, *extra)
        _post_mixer(y, xt, mod, ngv, rwh_ref, rwl_ref, rb_ref, base_ref,
                    x1_ref, h2_ref, mi_ref, mf_ref, cnt_ref)

    in_specs = [
        pl.BlockSpec((1, TM, d), lambda bi, j: (bi, j, 0)),
        pl.BlockSpec((1, SUBLANES, d), lambda bi, j: (bi * 2 + jnp.minimum(j, 1), 0, 0)),
        pl.BlockSpec((4, d), lambda bi, j: (0, 0)),
        pl.BlockSpec((d, LANES), lambda bi, j: (0, 0)),
        pl.BlockSpec((d, LANES), lambda bi, j: (0, 0)),
        pl.BlockSpec((1, LANES), lambda bi, j: (0, 0)),
    ] + list(extra_specs)
    out_specs = [
        pl.BlockSpec((1, TM, d), lambda bi, j: (bi, j, 0)),
        pl.BlockSpec((TM, d), lambda bi, j: (bi * nt + j, 0)),
        pl.BlockSpec((TM, LANES), lambda bi, j: (bi * nt + j, 0)),
        pl.BlockSpec((TM, LANES), lambda bi, j: (bi * nt + j, 0)),
        pl.BlockSpec((1, LANES), lambda bi, j: (0, 0)),
    ]
    out_shape = [
        jax.ShapeDtypeStruct((b, s, d), F32),
        jax.ShapeDtypeStruct((b * s, d), F32),
        jax.ShapeDtypeStruct((b * s, LANES), jnp.int32),
        jax.ShapeDtypeStruct((b * s, LANES), F32),
        jax.ShapeDtypeStruct((1, LANES), F32),
    ]
    return pl.pallas_call(
        kern, grid=(b, nt), in_specs=in_specs, out_specs=out_specs, out_shape=out_shape,
        scratch_shapes=[pltpu.VMEM((1, LANES), F32)],
        compiler_params=_params("arbitrary", "arbitrary"), name=name,
    )(x, modt, ng, rwh, rwl, rb, *extra_args)


def _halo_specs(s, d):
    last = s // HALO - 1
    per_tile = TM // HALO
    prev = pl.BlockSpec((1, HALO, d), lambda bi, j: (bi, jnp.maximum(j * per_tile - 1, 0), 0))
    nxt = pl.BlockSpec((1, HALO, d), lambda bi, j: (bi, jnp.minimum((j + 1) * per_tile, last), 0))
    return prev, nxt


def _halo_valid():
    j = pl.program_id(1)
    nt = pl.num_programs(1)
    return j >= 2, (j >= 1) & (j <= nt - 2)


def _pool_body(ctx_len, lat_len, xt, mod, ng, xp_ref, xn_ref, pw_ref, ps_ref):
    tm, d = xt.shape
    group = d // len(POOL_WINDOWS)
    prev_ok, next_ok = _halo_valid()
    pre = lambda rows: _rms(rows, ng[0:1]) * (1.0 + mod[1:2]) + mod[0:1]
    hc = pre(xt)
    hp = jnp.where(prev_ok, pre(xp_ref[0]), 0.0)
    hn = jnp.where(next_ok, pre(xn_ref[0]), 0.0)
    ext = jnp.concatenate([hp, hc, hn], axis=0)

    j = pl.program_id(1)
    pos = lax.broadcasted_iota(jnp.int32, (tm, 1), 0) + jnp.where(j == 0, 0, (j - 1) * tm)
    seq_len = jnp.where(j == 0, ctx_len, lat_len)
    outs = []
    for g, w in enumerate(POOL_WINDOWS):
        e = ext[:, g * group:(g + 1) * group]
        run, span = e, 1
        while span < w:
            run = run + _shift_up(run, span)
            span *= 2
        win = _shift_up(run, HALO - w // 2)[:tm] if w // 2 != HALO else run[:tm]
        lo = jnp.maximum(pos - w // 2, 0)
        hi = jnp.minimum(pos + (w - w // 2), seq_len)
        diff = win / (hi - lo).astype(F32) - hc[:, g * group:(g + 1) * group]
        outs.append(_dot(diff, pw_ref[g]))
    return jnp.concatenate(outs, axis=1) * ps_ref[...]


def _pool_layer(x, modt, ng, router, pool_w, pool_scale, ctx_len):
    b, s, d = x.shape
    prev, nxt = _halo_specs(s, d)
    g = len(POOL_WINDOWS)
    specs = [prev, nxt,
             pl.BlockSpec((g, d // g, d // g), lambda bi, j: (0, 0, 0)),
             pl.BlockSpec((1, d), lambda bi, j: (0, 0))]
    body = functools.partial(_pool_body, ctx_len, s - ctx_len)
    return _mixer_call(body, x, modt, ng, router,
                       (x, x, pool_w.astype(MXU_DTYPE), pool_scale.reshape(1, d)), specs, "pool_mixer")


def _conv_body(xt, mod, ng, xp_ref, xn_ref, win_ref, wdw_ref, wout_ref):
    tm, d = xt.shape
    prev_ok, next_ok = _halo_valid()
    rows = jnp.concatenate([xp_ref[0], xt, xn_ref[0]], axis=0)
    h = _rms(rows, ng[0:1]) * (1.0 + mod[1:2]) + mod[0:1]
    proj = _dot(h, win_ref[...])
    gate_b = proj[HALO:HALO + tm, :d]
    u = proj[:, d:2 * d] * proj[:, 2 * d:]
    r = lax.broadcasted_iota(jnp.int32, (tm + 2 * HALO, 1), 0)
    keep = ((r >= HALO) | prev_ok) & ((r < HALO + tm) | next_ok)
    u = jnp.where(keep, u, 0.0)
    wdw = wdw_ref[...]
    conv = (wdw[0:1] * _shift_up(u, HALO - 1)[:tm] + wdw[1:2] * u[HALO:HALO + tm]
            + wdw[2:3] * _shift_up(u, HALO + 1)[:tm])
    return _dot(gate_b * conv, wout_ref[...])


def _conv_layer(x, modt, ng, router, w_in, w_dw, w_out):
    b, s, d = x.shape
    prev, nxt = _halo_specs(s, d)
    specs = [prev, nxt,
             pl.BlockSpec((d, 3 * d), lambda bi, j: (0, 0)),
             pl.BlockSpec((SUBLANES, d), lambda bi, j: (0, 0)),
             pl.BlockSpec((d, d), lambda bi, j: (0, 0))]
    w_dw8 = jnp.zeros((SUBLANES, d), F32).at[:w_dw.shape[0]].set(w_dw)
    return _mixer_call(_conv_body, x, modt, ng, router,
                       (x, x, w_in.astype(MXU_DTYPE), w_dw8, w_out.astype(MXU_DTYPE)), specs, "conv_mixer")


def _gqa_qkv_kernel(x_ref, mod_ref, ng_ref, w_ref, qg_ref, kg_ref, cos_ref, sin_ref,
                    q_ref, k_ref, v_ref):
    mod = mod_ref[0]
    h = _rms(x_ref[0], ng_ref[0:1]) * (1.0 + mod[1:2]) + mod[0:1]
    qkv = _dot(h, w_ref[...])
    cos, sin = cos_ref[...], sin_ref[...]
    hd = GQA_HEAD_DIM

    def rope(t):
        return t * cos + pltpu.roll(t, hd // 2, 1) * sin

    scale = hd ** -0.5
    for i in range(GQA_Q_HEADS):
        q = rope(_rms(qkv[:, i * hd:(i + 1) * hd], qg_ref[...]))
        q_ref[0, i] = (q * scale).astype(q_ref.dtype)
    for i in range(GQA_KV_HEADS):
        o = (GQA_Q_HEADS + i) * hd
        k_ref[0, i] = rope(_rms(qkv[:, o:o + hd], kg_ref[...])).astype(k_ref.dtype)
        o = (GQA_Q_HEADS + GQA_KV_HEADS + i) * hd
        v_ref[0, i] = qkv[:, o:o + hd].astype(v_ref.dtype)


def _gqa_qkv(x, modt, ng, w_qkv, q_gain, k_gain, cos, sin):
    b, s, d = x.shape
    hd = GQA_HEAD_DIM
    n = w_qkv.shape[1]
    return pl.pallas_call(
        _gqa_qkv_kernel, grid=(b, s // TM),
        in_specs=[pl.BlockSpec((1, TM, d), lambda bi, j: (bi, j, 0)),
                  pl.BlockSpec((1, SUBLANES, d), lambda bi, j: (bi * 2 + jnp.minimum(j, 1), 0, 0)),
                  pl.BlockSpec((4, d), lambda bi, j: (0, 0)),
                  pl.BlockSpec((d, n), lambda bi, j: (0, 0)),
                  pl.BlockSpec((1, hd), lambda bi, j: (0, 0)),
                  pl.BlockSpec((1, hd), lambda bi, j: (0, 0)),
                  pl.BlockSpec((TM, hd), lambda bi, j: (j, 0)),
                  pl.BlockSpec((TM, hd), lambda bi, j: (j, 0))],
        out_specs=[pl.BlockSpec((1, GQA_Q_HEADS, TM, hd), lambda bi, j: (bi, 0, j, 0)),
                   pl.BlockSpec((1, GQA_KV_HEADS, TM, hd), lambda bi, j: (bi, 0, j, 0)),
                   pl.BlockSpec((1, GQA_KV_HEADS, TM, hd), lambda bi, j: (bi, 0, j, 0))],
        out_shape=[jax.ShapeDtypeStruct((b, GQA_Q_HEADS, s, hd), MXU_DTYPE),
                   jax.ShapeDtypeStruct((b, GQA_KV_HEADS, s, hd), MXU_DTYPE),
                   jax.ShapeDtypeStruct((b, GQA_KV_HEADS, s, hd), MXU_DTYPE)],
        compiler_params=_params("arbitrary", "arbitrary"), name="gqa_qkv",
    )(x, modt, ng, w_qkv.astype(MXU_DTYPE), q_gain.reshape(1, hd), k_gain.reshape(1, hd), cos, sin)


def _diff_qkv_kernel(x_ref, mod_ref, ng_ref, w_ref, cos_ref, sin_ref, q_ref, k_ref, v_ref):
    mod = mod_ref[0]
    d = x_ref.shape[2]
    h = _rms(x_ref[0], ng_ref[0:1]) * (1.0 + mod[1:2]) + mod[0:1]
    qkv = _dot(h, w_ref[...])
    cos, sin = cos_ref[...], sin_ref[...]
    hw = 2 * DIFF_HEAD_DIM
    lane = lax.broadcasted_iota(jnp.int32, (x_ref.shape[1], hw), 1)
    quarter = DIFF_HEAD_DIM // 2
    take_up = (lane // quarter) % 2 == 0

    def rope(t):
        rot = jnp.where(take_up, pltpu.roll(t, hw - quarter, 1), pltpu.roll(t, quarter, 1))
        return t * cos + rot * sin

    scale = DIFF_HEAD_DIM ** -0.5
    first = lane < DIFF_HEAD_DIM
    for i in range(DIFF_HEADS):
        q = rope(qkv[:, i * hw:(i + 1) * hw]) * scale
        q_ref[0, 2 * i] = jnp.where(first, q, 0.0).astype(q_ref.dtype)
        q_ref[0, 2 * i + 1] = jnp.where(first, 0.0, q).astype(q_ref.dtype)
        k_ref[0, i] = rope(qkv[:, d + i * hw:d + (i + 1) * hw]).astype(k_ref.dtype)
        v_ref[0, i] = qkv[:, 2 * d + i * hw:2 * d + (i + 1) * hw].astype(v_ref.dtype)


def _diff_qkv(x, modt, ng, w_qkv, cos, sin):
    b, s, d = x.shape
    hw = 2 * DIFF_HEAD_DIM
    n = w_qkv.shape[1]
    return pl.pallas_call(
        _diff_qkv_kernel, grid=(b, s // TM),
        in_specs=[pl.BlockSpec((1, TM, d), lambda bi, j: (bi, j, 0)),
                  pl.BlockSpec((1, SUBLANES, d), lambda bi, j: (bi * 2 + jnp.minimum(j, 1), 0, 0)),
                  pl.BlockSpec((4, d), lambda bi, j: (0, 0)),
                  pl.BlockSpec((d, n), lambda bi, j: (0, 0)),
                  pl.BlockSpec((TM, hw), lambda bi, j: (j, 0)),
                  pl.BlockSpec((TM, hw), lambda bi, j: (j, 0))],
        out_specs=[pl.BlockSpec((1, 2 * DIFF_HEADS, TM, hw), lambda bi, j: (bi, 0, j, 0)),
                   pl.BlockSpec((1, DIFF_HEADS, TM, hw), lambda bi, j: (bi, 0, j, 0)),
                   pl.BlockSpec((1, DIFF_HEADS, TM, hw), lambda bi, j: (bi, 0, j, 0))],
        out_shape=[jax.ShapeDtypeStruct((b, 2 * DIFF_HEADS, s, hw), MXU_DTYPE),
                   jax.ShapeDtypeStruct((b, DIFF_HEADS, s, hw), MXU_DTYPE),
                   jax.ShapeDtypeStruct((b, DIFF_HEADS, s, hw), MXU_DTYPE)],
        compiler_params=_params("arbitrary", "arbitrary"), name="diff_qkv",
    )(x, modt, ng, w_qkv.astype(MXU_DTYPE), cos, sin)


def _flash_kernel(*refs, group, ctx_len, kv_chunk, n_chunks, lam_init):
    if lam_init is None:
        q_ref, k_ref, v_ref, o_ref = refs
    else:
        q_ref, k_ref, v_ref, lam_ref, sg_ref, o_ref = refs
    tq, hd = q_ref.shape[2], q_ref.shape[3]
    j = pl.program_id(2)
    q = q_ref[0].reshape(group * tq, hd)

    def step(carry, start, size):
        m, l, acc = carry
        k = k_ref[0, 0, pl.ds(start, size), :]
        v = v_ref[0, 0, pl.ds(start, size), :]
        s = lax.dot_general(q, k, (((1,), (1,)), ((), ())), preferred_element_type=F32)
        m_new = jnp.maximum(m, jnp.max(s, axis=-1, keepdims=True))
        alpha = jnp.exp(m - m_new)
        p = jnp.exp(s - m_new)
        l = alpha * l + jnp.sum(p, axis=-1, keepdims=True)
        acc = alpha * acc + jnp.dot(p.astype(v.dtype), v, preferred_element_type=F32)
        return m_new, l, acc

    rows = group * tq
    carry = (jnp.full((rows, 1), NEG_BIG, F32), jnp.zeros((rows, 1), F32), jnp.zeros((rows, hd), F32))
    carry = step(carry, 0, ctx_len)
    n = jnp.where(j > 0, n_chunks, 0)
    carry = lax.fori_loop(
        0, n, lambda i, c: step(c, pl.multiple_of(ctx_len + i * kv_chunk, TM), kv_chunk), carry)
    _, l, acc = carry
    o = acc / l
    if lam_init is None:
        for g in range(group):
            o_ref[0, :, g * hd:(g + 1) * hd] = o[g * tq:(g + 1) * tq].astype(o_ref.dtype)
    else:
        lv = lam_ref[...]
        lam = (jnp.exp(jnp.sum(lv[0:1] * lv[1:2], axis=-1, keepdims=True))
               - jnp.exp(jnp.sum(lv[2:3] * lv[3:4], axis=-1, keepdims=True)) + lam_init)
        od = o[:tq] - lam * o[tq:]
        o_ref[0] = (_rms(od, sg_ref[...]) * (1.0 - lam_init)).astype(o_ref.dtype)


def _flash(q, k, v, group, ctx_len, lam=None, subln=None, lam_init=None):
    b, hq, s, hd = q.shape
    hkv = k.shape[1]
    lat = s - ctx_len
    kv_chunk = min(KV_CHUNK, lat)
    kern = functools.partial(_flash_kernel, group=group, ctx_len=ctx_len, kv_chunk=kv_chunk,
                             n_chunks=lat // kv_chunk, lam_init=lam_init)
    in_specs = [pl.BlockSpec((1, group, TM, hd), lambda bi, h, j: (bi, h, j, 0)),
                pl.BlockSpec((1, 1, s, hd), lambda bi, h, j: (bi, h, 0, 0)),
                pl.BlockSpec((1, 1, s, hd), lambda bi, h, j: (bi, h, 0, 0))]
    args = [q, k, v]
    if lam_init is None:
        out_w = group * hd
    else:
        out_w = hd
        in_specs += [pl.BlockSpec(lam.shape, lambda bi, h, j: (0, 0)),
                     pl.BlockSpec((1, hd), lambda bi, h, j: (0, 0))]
        args += [lam, subln.reshape(1, hd)]
    return pl.pallas_call(
        kern, grid=(b, hkv, s // TM), in_specs=in_specs,
        out_specs=pl.BlockSpec((1, TM, out_w), lambda bi, h, j: (bi, j, h)),
        out_shape=jax.ShapeDtypeStruct((b, s, hkv * out_w), MXU_DTYPE),
        compiler_params=_params("arbitrary", "arbitrary", "arbitrary"), name="flash_attention",
    )(*args)


def _attn_out_body(xt, mod, ng, o_ref, wo_ref):
    return _dot(o_ref[0], wo_ref[...])


def _attn_out_layer(x, modt, ng, router, o, w_o):
    b, s, d = x.shape
    specs = [pl.BlockSpec((1, TM, d), lambda bi, j: (bi, j, 0)),
             pl.BlockSpec((d, d), lambda bi, j: (0, 0))]
    return _mixer_call(_attn_out_body, x, modt, ng, router, (o, w_o.astype(MXU_DTYPE)), specs, "attn_out")


def _rope_tables(ctx_len, lat_len, dim, reps):
    n = dim // 4
    pos = jnp.arange(lat_len, dtype=jnp.int32)
    row = (pos // GRID_W).astype(F32)
    col = (pos % GRID_W).astype(F32)
    freqs = ROPE_THETA ** (-jnp.arange(n, dtype=F32) / n)
    ang = jnp.concatenate([row[:, None] * freqs, col[:, None] * freqs], axis=-1)
    ang = jnp.concatenate([jnp.zeros((ctx_len, dim // 2), F32), ang], axis=0)
    cos, sin = jnp.cos(ang), jnp.sin(ang)
    return jnp.tile(cos, (1, 2 * reps)), jnp.tile(jnp.concatenate([-sin, sin], axis=-1), (1, reps))


def _dispatch_kernel(dest_ref, h_ref, zeros_hbm, hs_hbm, sem):
    del zeros_hbm

    def row_copy(src_row, dst_row):
        return pltpu.make_async_copy(h_ref.at[pl.ds(src_row, 1)], hs_hbm.at[pl.ds(dst_row, 1)], sem)

    def issue(r, c):
        for k in range(TOP_K):
            row_copy(r, dest_ref[0, 0, r * TOP_K + k]).start()
        return c

    def drain(r, c):
        for k in range(TOP_K):
            row_copy(r, dest_ref[0, 0, r * TOP_K + k]).wait()
        return c

    lax.fori_loop(0, TM, issue, 0)
    lax.fori_loop(0, TM, drain, 0)


def _dispatch(h2, dest3, n_pad):
    t, d = h2.shape
    return pl.pallas_call(
        _dispatch_kernel, grid=(t // TM,),
        in_specs=[pl.BlockSpec((1, 1, TM * TOP_K), lambda i: (i, 0, 0), memory_space=pltpu.SMEM),
                  pl.BlockSpec((TM, d), lambda i: (i, 0)),
                  pl.BlockSpec(memory_space=pl.ANY)],
        out_specs=pl.BlockSpec(memory_space=pl.ANY),
        out_shape=jax.ShapeDtypeStruct((n_pad, d), F32),
        scratch_shapes=[pltpu.SemaphoreType.DMA],
        input_output_aliases={2: 0},
        compiler_params=pltpu.CompilerParams(dimension_semantics=("arbitrary",), has_side_effects=True),
        name="moe_dispatch",
    )(dest3, h2, jnp.zeros((n_pad, d), F32))


def _expert_kernel(be_ref, nused_ref, hs_ref, w1_ref, b1_ref, w2_ref, b2_ref, o_ref):
    del be_ref
    i = pl.program_id(0)
    f = w2_ref.shape[1]

    @pl.when(i < nused_ref[0])
    def _():
        gu = _dot(hs_ref[...], w1_ref[0]) + b1_ref[0]
        g = jnp.minimum(gu[:, :f], SWIGLU_LIMIT)
        u = jnp.clip(gu[:, f:], -SWIGLU_LIMIT, SWIGLU_LIMIT)
        act = g * _sigmoid(SWIGLU_ALPHA * g) * (u + 1.0)
        o_ref[...] = _dot(act, w2_ref[0]) + b2_ref[0]

    @pl.when(i >= nused_ref[0])
    def _():
        o_ref[...] = jnp.zeros_like(o_ref)


def _experts(hs, block_e, n_used, w1, b1, w2, b2):
    n_pad, d = hs.shape
    e, _, f2 = w1.shape
    f = w2.shape[1]
    tb = EXPERT_ROWS
    grid_spec = pltpu.PrefetchScalarGridSpec(
        num_scalar_prefetch=2, grid=(n_pad // tb,),
        in_specs=[pl.BlockSpec((tb, d), lambda i, be, nu: (i, 0)),
                  pl.BlockSpec((1, d, f2), lambda i, be, nu: (be[i], 0, 0)),
                  pl.BlockSpec((1, 1, f2), lambda i, be, nu: (be[i], 0, 0)),
                  pl.BlockSpec((1, f, d), lambda i, be, nu: (be[i], 0, 0)),
                  pl.BlockSpec((1, 1, d), lambda i, be, nu: (be[i], 0, 0))],
        out_specs=pl.BlockSpec((tb, d), lambda i, be, nu: (i, 0)))
    return pl.pallas_call(
        _expert_kernel, grid_spec=grid_spec,
        out_shape=jax.ShapeDtypeStruct((n_pad, d), F32),
        compiler_params=_params("arbitrary"), name="moe_experts",
    )(block_e, n_used, hs, w1, b1.reshape(e, 1, f2), w2, b2.reshape(e, 1, d))


def _combine_kernel(dest_ref, ys_hbm, mf_ref, x1_ref, mod_ref, ng_ref, x2_ref, buf, sem):
    def row_copy(src_row, k, r):
        return pltpu.make_async_copy(ys_hbm.at[pl.ds(src_row, 1)], buf.at[k, pl.ds(r, 1)], sem)

    def issue(r, c):
        for k in range(TOP_K):
            row_copy(dest_ref[0, 0, r * TOP_K + k], k, r).start()
        return c

    def drain(r, c):
        for k in range(TOP_K):
            row_copy(dest_ref[0, 0, r * TOP_K + k], k, r).wait()
        return c

    lax.fori_loop(0, TM, issue, 0)
    lax.fori_loop(0, TM, drain, 0)
    gates = mf_ref[...]
    y = gates[:, 0:1] * buf[0]
    for k in range(1, TOP_K):
        y = y + gates[:, k:k + 1] * buf[k]
    mod = mod_ref[0]
    x2_ref[0] = x1_ref[0] + mod[5:6] * _rms(y, ng_ref[3:4])


def _combine(ys, dest3, mf, x1, modt, ng):
    b, s, d = x1.shape
    nt = s // TM
    return pl.pallas_call(
        _combine_kernel, grid=(b, nt),
        in_specs=[pl.BlockSpec((1, 1, TM * TOP_K), lambda bi, j: (bi * nt + j, 0, 0), memory_space=pltpu.SMEM),
                  pl.BlockSpec(memory_space=pl.ANY),
                  pl.BlockSpec((TM, LANES), lambda bi, j: (bi * nt + j, 0)),
                  pl.BlockSpec((1, TM, d), lambda bi, j: (bi, j, 0)),
                  pl.BlockSpec((1, SUBLANES, d), lambda bi, j: (bi * 2 + jnp.minimum(j, 1), 0, 0)),
                  pl.BlockSpec((4, d), lambda bi, j: (0, 0))],
        out_specs=pl.BlockSpec((1, TM, d), lambda bi, j: (bi, j, 0)),
        out_shape=jax.ShapeDtypeStruct((b, s, d), F32),
        scratch_shapes=[pltpu.VMEM((TOP_K, TM, d), F32), pltpu.SemaphoreType.DMA],
        compiler_params=_params("arbitrary", "arbitrary"), name="moe_combine",
    )(dest3, ys, mf, x1, modt, ng)


def _moe(x1, h2, mi, mf, cnt, modt, ng, w1, b1, w2, b2):
    t = h2.shape[0]
    tb = EXPERT_ROWS
    counts = cnt[0, :N_EXPERTS].astype(jnp.int32)
    padded = (counts + tb - 1) // tb * tb
    pend = jnp.cumsum(padded)
    pstart = pend - padded
    dest = pstart[mi[:, :TOP_K]] + mi[:, TOP_K:2 * TOP_K]
    n_blocks = -(-(t * TOP_K) // tb) + N_EXPERTS
    block_row0 = jnp.arange(n_blocks, dtype=jnp.int32) * tb
    block_e = jnp.minimum(jnp.sum((pend[None, :] <= block_row0[:, None]).astype(jnp.int32), axis=1),
                          N_EXPERTS - 1)
    n_used = (pend[-1:] // tb).astype(jnp.int32)
    dest3 = dest.reshape(t // TM, 1, TM * TOP_K)
    hs = _dispatch(h2, dest3, n_blocks * tb)
    ys = _experts(hs, block_e, n_used, w1, b1, w2, b2)
    return _combine(ys, dest3, mf, x1, modt, ng)


def kernel(x, c, ctx, c_ctx, mod_w, mod_b, norm_g, pool_w, pool_scale, gqa_w_qkv, gqa_w_o, gqa_q_gain, gqa_k_gain, conv_w_in, conv_w_dw, conv_w_out, diff_w_qkv, diff_w_o, diff_lambda, diff_subln_gain, router_w, router_b, moe_w1, moe_b1, moe_w2, moe_b2):
    b, lat_len, d = x.shape
    ctx_len = ctx.shape[1]
    depth = mod_w.shape[0]
    n_mixers = 4
    assert ctx_len == TM and lat_len % TM == 0 and b + 1 <= SUBLANES and d % LANES == 0

    xa = jnp.concatenate([ctx, x], axis=1)

    cv = jnp.zeros((SUBLANES, d), F32).at[:b].set(c).at[b].set(c_ctx)
    mods = _modulation(cv, mod_w, mod_b).reshape(depth, SUBLANES, MOD_CHUNKS, d)
    modt_all = jnp.stack([jnp.broadcast_to(mods[:, b][:, None], (depth, b, MOD_CHUNKS, d)), mods[:, :b]], axis=2)
    modt_all = jnp.pad(modt_all, ((0, 0), (0, 0), (0, 0), (0, SUBLANES - MOD_CHUNKS), (0, 0)))
    modt_all = modt_all.reshape(depth, b * 2, SUBLANES, d)

    cos_b, sin_b = _rope_tables(ctx_len, lat_len, GQA_HEAD_DIM, 1)
    cos_d, sin_d = _rope_tables(ctx_len, lat_len, DIFF_HEAD_DIM, 2)

    for i in range(depth):
        m, jj = i % n_mixers, i // n_mixers
        modt, ng = modt_all[i], norm_g[i]
        rw = jnp.zeros((d, LANES), F32).at[:, :N_EXPERTS].set(router_w[i])
        rwh = rw.astype(jnp.bfloat16)
        rwl = (rw - rwh.astype(F32)).astype(jnp.bfloat16)
        rb = jnp.full((1, LANES), NEG_BIG, F32).at[0, :N_EXPERTS].set(router_b[i])
        router = (rwh, rwl, rb)
        if m == 0:
            res = _pool_layer(xa, modt, ng, router, pool_w[jj], pool_scale[jj], ctx_len)
        elif m == 1:
            q, k, v = _gqa_qkv(xa, modt, ng, gqa_w_qkv[jj], gqa_q_gain[jj], gqa_k_gain[jj], cos_b, sin_b)
            o = _flash(q, k, v, GQA_Q_HEADS // GQA_KV_HEADS, ctx_len)
            res = _attn_out_layer(xa, modt, ng, router, o, gqa_w_o[jj])
        elif m == 2:
            res = _conv_layer(xa, modt, ng, router, conv_w_in[jj], conv_w_dw[jj], conv_w_out[jj])
        else:
            lam_init = 0.8 - 0.6 * math.exp(-0.3 * i)
            q, k, v = _diff_qkv(xa, modt, ng, diff_w_qkv[jj], cos_d, sin_d)
            o = _flash(q, k, v, 2, ctx_len, lam=diff_lambda[jj], subln=diff_subln_gain[jj], lam_init=lam_init)
            res = _attn_out_layer(xa, modt, ng, router, o, diff_w_o[jj])
        x1, h2, mi, mf, cnt = res
        xa = _moe(x1, h2, mi, mf, cnt, modt, ng,
                  moe_w1[i].astype(MXU_DTYPE), moe_b1[i], moe_w2[i].astype(MXU_DTYPE), moe_b2[i])
    return xa[:, ctx_len:]
```

```python
import functools
import math

import jax
import jax.numpy as jnp
from jax import lax
from jax.experimental import pallas as pl
from jax.experimental.pallas import tpu as pltpu

RMS_EPS = 1e-6
ROPE_THETA = 10000.0
GRID_W = 64
POOL_WINDOWS = (2, 4, 8, 16)
N_EXPERTS = 32
TOP_K = 4
SWIGLU_LIMIT = 7.0
SWIGLU_ALPHA = 1.702
GQA_HEAD_DIM = 128
GQA_Q_HEADS = 8
GQA_KV_HEADS = 2
DIFF_HEAD_DIM = 64
DIFF_HEADS = 8
MOD_CHUNKS = 6

LANES = 128
SUBLANES = 8
VMEM_LIMIT_BYTES = 56 * 1024 * 1024

TM = 256
EXPERT_ROWS = 512
KV_CHUNK = 1024
SCORE_COLS = 256
LOG2E = math.log2(math.e)
HALO = SUBLANES

MXU_DTYPE = jnp.bfloat16
F32 = jnp.float32
NEG_BIG = -1e30


def _params(*sem):
    return pltpu.CompilerParams(dimension_semantics=sem, vmem_limit_bytes=VMEM_LIMIT_BYTES)


def _dot(a, b):
    return jnp.dot(a.astype(MXU_DTYPE), b.astype(MXU_DTYPE), preferred_element_type=F32)


def _rms(x, g):
    return x * lax.rsqrt(jnp.mean(x * x, axis=-1, keepdims=True) + RMS_EPS) * g


def _sigmoid(z):
    return 1.0 / (1.0 + jnp.exp(-z))


def _shift_up(a, s):
    n = a.shape[0]
    return pltpu.roll(a, (n - s) % n, 0)


def _mod_kernel(cv_ref, w_ref, b_ref, o_ref):
    a = cv_ref[...]
    o_ref[0] = _dot(a * _sigmoid(a), w_ref[0]) + b_ref[0]


def _modulation(cv, mod_w, mod_b):
    depth, d, n = mod_w.shape
    tn = n // 4
    return pl.pallas_call(
        _mod_kernel,
        grid=(depth, n // tn),
        in_specs=[pl.BlockSpec((SUBLANES, d), lambda i, k: (0, 0)),
                  pl.BlockSpec((1, d, tn), lambda i, k: (i, 0, k)),
                  pl.BlockSpec((1, 1, tn), lambda i, k: (i, 0, k))],
        out_specs=pl.BlockSpec((1, SUBLANES, tn), lambda i, k: (i, 0, k)),
        out_shape=jax.ShapeDtypeStruct((depth, SUBLANES, n), F32),
        compiler_params=_params("arbitrary", "arbitrary"),
        name="modulation",
    )(cv, mod_w, mod_b.reshape(depth, 1, n))


def _post_mixer(y, x, mod, ng, rwh_ref, rwl_ref, rb_ref, base_ref,
                x1_ref, h2_ref, mi_ref, mf_ref, cnt_ref):
    tm = x.shape[0]
    x1 = x + mod[2:3] * _rms(y, ng[1:2])
    h2 = _rms(x1, ng[2:3]) * (1.0 + mod[4:5]) + mod[3:4]
    x1_ref[0] = x1
    h2_ref[...] = h2

    h_hi = h2.astype(jnp.bfloat16)
    h_lo = (h2 - h_hi.astype(F32)).astype(jnp.bfloat16)
    rwh = rwh_ref[...]
    logits = (jnp.dot(h_hi, rwh, preferred_element_type=F32)
              + jnp.dot(h_lo, rwh, preferred_element_type=F32)
              + jnp.dot(h_hi, rwl_ref[...], preferred_element_type=F32)) + rb_ref[...]

    lane = lax.broadcasted_iota(jnp.int32, (tm, LANES), 1)
    lane_f = lane.astype(F32)
    vals, idxs, hots = [], [], []
    l = logits
    for _ in range(TOP_K):
        m = jnp.max(l, axis=-1, keepdims=True)
        idx = jnp.min(jnp.where(l == m, lane_f, float(LANES)), axis=-1, keepdims=True)
        hot = lane_f == idx
        vals.append(m)
        idxs.append(idx)
        hots.append(hot)
        l = jnp.where(hot, -3e38, l)
    exps = [jnp.exp(v - vals[0]) for v in vals]
    den = exps[0] + exps[1] + exps[2] + exps[3]

    @pl.when((pl.program_id(0) == 0) & (pl.program_id(1) == 0))
    def _():
        base_ref[...] = jnp.zeros_like(base_ref)

    hot_all = (hots[0].astype(F32) + hots[1].astype(F32) + hots[2].astype(F32) + hots[3].astype(F32))
    rows = lax.broadcasted_iota(jnp.int32, (tm, tm), 0)
    cols = lax.broadcasted_iota(jnp.int32, (tm, tm), 1)
    lower = (rows > cols).astype(jnp.bfloat16)
    before = jnp.dot(lower, hot_all.astype(jnp.bfloat16), preferred_element_type=F32) + base_ref[...]
    mi = jnp.zeros((tm, LANES), F32)
    mf = jnp.zeros((tm, LANES), F32)
    for k in range(TOP_K):
        rank = jnp.sum(jnp.where(hots[k], before, 0.0), axis=-1, keepdims=True)
        mi = mi + jnp.where(lane == k, idxs[k], 0.0) + jnp.where(lane == TOP_K + k, rank, 0.0)
        mf = mf + jnp.where(lane == k, exps[k] / den, 0.0)
    mi_ref[...] = mi.astype(jnp.int32)
    mf_ref[...] = mf
    base_ref[...] = base_ref[...] + jnp.sum(hot_all, axis=0, keepdims=True)
    cnt_ref[...] = base_ref[...]


def _mixer_call(body, x, modt, ng, router, extra_args, extra_specs, name):
    b, s, d = x.shape
    nt = s // TM
    rwh, rwl, rb = router
    n_extra = len(extra_args)

    def kern(*refs):
        x_ref, mod_ref, ng_ref, rwh_ref, rwl_ref, rb_ref = refs[:6]
        extra = refs[6:6 + n_extra]
        x1_ref, h2_ref, mi_ref, mf_ref, cnt_ref, base_ref = refs[6 + n_extra:]
        xt = x_ref[0]
        mod = mod_ref[0]
        ngv = ng_ref[...]
        y = body(xt, mod, ngv, *extra)
        _post_mixer(y, xt, mod, ngv, rwh_ref, rwl_ref, rb_ref, base_ref,
                    x1_ref, h2_ref, mi_ref, mf_ref, cnt_ref)

    in_specs = [
        pl.BlockSpec((1, TM, d), lambda bi, j: (bi, j, 0)),
        pl.BlockSpec((1, SUBLANES, d), lambda bi, j: (bi * 2 + jnp.minimum(j, 1), 0, 0)),
        pl.BlockSpec((4, d), lambda bi, j: (0, 0)),
        pl.BlockSpec((d, LANES), lambda bi, j: (0, 0)),
        pl.BlockSpec((d, LANES), lambda bi, j: (0, 0)),
        pl.BlockSpec((1, LANES), lambda bi, j: (0, 0)),
    ] + list(extra_specs)
    out_specs = [
        pl.BlockSpec((1, TM, d), lambda bi, j: (bi, j, 0)),
        pl.BlockSpec((TM, d), lambda bi, j: (bi * nt + j, 0)),
        pl.BlockSpec((TM, LANES), lambda bi, j: (bi * nt + j, 0)),
        pl.BlockSpec((TM, LANES), lambda bi, j: (bi * nt + j, 0)),
        pl.BlockSpec((1, LANES), lambda bi, j: (0, 0)),
    ]
    out_shape = [
        jax.ShapeDtypeStruct((b, s, d), F32),
        jax.ShapeDtypeStruct((b * s, d), F32),
        jax.ShapeDtypeStruct((b * s, LANES), jnp.int32),
        jax.ShapeDtypeStruct((b * s, LANES), F32),
        jax.ShapeDtypeStruct((1, LANES), F32),
    ]
    return pl.pallas_call(
        kern, grid=(b, nt), in_specs=in_specs, out_specs=out_specs, out_shape=out_shape,
        scratch_shapes=[pltpu.VMEM((1, LANES), F32)],
        compiler_params=_params("arbitrary", "arbitrary"), name=name,
    )(x, modt, ng, rwh, rwl, rb, *extra_args)


def _halo_specs(s, d):
    last = s // HALO - 1
    per_tile = TM // HALO
    prev = pl.BlockSpec((1, HALO, d), lambda bi, j: (bi, jnp.maximum(j * per_tile - 1, 0), 0))
    nxt = pl.BlockSpec((1, HALO, d), lambda bi, j: (bi, jnp.minimum((j + 1) * per_tile, last), 0))
    return prev, nxt


def _halo_valid():
    j = pl.program_id(1)
    nt = pl.num_programs(1)
    return j >= 2, (j >= 1) & (j <= nt - 2)


def _pool_body(ctx_len, lat_len, xt, mod, ng, xp_ref, xn_ref, pw_ref, ps_ref):
    tm, d = xt.shape
    group = d // len(POOL_WINDOWS)
    prev_ok, next_ok = _halo_valid()
    pre = lambda rows: _rms(rows, ng[0:1]) * (1.0 + mod[1:2]) + mod[0:1]
    hc = pre(xt)
    hp = jnp.where(prev_ok, pre(xp_ref[0]), 0.0)
    hn = jnp.where(next_ok, pre(xn_ref[0]), 0.0)
    ext = jnp.concatenate([hp, hc, hn], axis=0)

    j = pl.program_id(1)
    pos = lax.broadcasted_iota(jnp.int32, (tm, 1), 0) + jnp.where(j == 0, 0, (j - 1) * tm)
    seq_len = jnp.where(j == 0, ctx_len, lat_len)
    outs = []
    for g, w in enumerate(POOL_WINDOWS):
        e = ext[:, g * group:(g + 1) * group]
        run, span = e, 1
        while span < w:
            run = run + _shift_up(run, span)
            span *= 2
        win = _shift_up(run, HALO - w // 2)[:tm] if w // 2 != HALO else run[:tm]
        lo = jnp.maximum(pos - w // 2, 0)
        hi = jnp.minimum(pos + (w - w // 2), seq_len)
        diff = win / (hi - lo).astype(F32) - hc[:, g * group:(g + 1) * group]
        outs.append(_dot(diff, pw_ref[g]))
    return jnp.concatenate(outs, axis=1) * ps_ref[...]


def _pool_layer(x, modt, ng, router, pool_w, pool_scale, ctx_len):
    b, s, d = x.shape
    prev, nxt = _halo_specs(s, d)
    g = len(POOL_WINDOWS)
    specs = [prev, nxt,
             pl.BlockSpec((g, d // g, d // g), lambda bi, j: (0, 0, 0)),
             pl.BlockSpec((1, d), lambda bi, j: (0, 0))]
    body = functools.partial(_pool_body, ctx_len, s - ctx_len)
    return _mixer_call(body, x, modt, ng, router,
                       (x, x, pool_w.astype(MXU_DTYPE), pool_scale.reshape(1, d)), specs, "pool_mixer")


def _conv_body(xt, mod, ng, xp_ref, xn_ref, win_ref, wdw_ref, wout_ref):
    tm, d = xt.shape
    prev_ok, next_ok = _halo_valid()
    rows = jnp.concatenate([xp_ref[0], xt, xn_ref[0]], axis=0)
    h = _rms(rows, ng[0:1]) * (1.0 + mod[1:2]) + mod[0:1]
    proj = _dot(h, win_ref[...])
    gate_b = proj[HALO:HALO + tm, :d]
    u = proj[:, d:2 * d] * proj[:, 2 * d:]
    r = lax.broadcasted_iota(jnp.int32, (tm + 2 * HALO, 1), 0)
    keep = ((r >= HALO) | prev_ok) & ((r < HALO + tm) | next_ok)
    u = jnp.where(keep, u, 0.0)
    wdw = wdw_ref[...]
    conv = (wdw[0:1] * _shift_up(u, HALO - 1)[:tm] + wdw[1:2] * u[HALO:HALO + tm]
            + wdw[2:3] * _shift_up(u, HALO + 1)[:tm])
    return _dot(gate_b * conv, wout_ref[...])


def _conv_layer(x, modt, ng, router, w_in, w_dw, w_out):
    b, s, d = x.shape
    prev, nxt = _halo_specs(s, d)
    specs = [prev, nxt,
             pl.BlockSpec((d, 3 * d), lambda bi, j: (0, 0)),
             pl.BlockSpec((SUBLANES, d), lambda bi, j: (0, 0)),
             pl.BlockSpec((d, d), lambda bi, j: (0, 0))]
    w_dw8 = jnp.zeros((SUBLANES, d), F32).at[:w_dw.shape[0]].set(w_dw)
    return _mixer_call(_conv_body, x, modt, ng, router,
                       (x, x, w_in.astype(MXU_DTYPE), w_dw8, w_out.astype(MXU_DTYPE)), specs, "conv_mixer")


def _gqa_qkv_kernel(x_ref, mod_ref, ng_ref, w_ref, qg_ref, kg_ref, cos_ref, sin_ref,
                    q_ref, k_ref, v_ref):
    mod = mod_ref[0]
    h = _rms(x_ref[0], ng_ref[0:1]) * (1.0 + mod[1:2]) + mod[0:1]
    qkv = _dot(h, w_ref[...])
    cos, sin = cos_ref[...], sin_ref[...]
    hd = GQA_HEAD_DIM

    def rope(t):
        return t * cos + pltpu.roll(t, hd // 2, 1) * sin

    scale = hd ** -0.5 * LOG2E
    for i in range(GQA_Q_HEADS):
        q = rope(_rms(qkv[:, i * hd:(i + 1) * hd], qg_ref[...]))
        q_ref[0, i] = (q * scale).astype(q_ref.dtype)
    for i in range(GQA_KV_HEADS):
        o = (GQA_Q_HEADS + i) * hd
        k_ref[0, i] = rope(_rms(qkv[:, o:o + hd], kg_ref[...])).T.astype(k_ref.dtype)
        o = (GQA_Q_HEADS + GQA_KV_HEADS + i) * hd
        v_ref[0, i] = qkv[:, o:o + hd].astype(v_ref.dtype)


def _gqa_qkv(x, modt, ng, w_qkv, q_gain, k_gain, cos, sin):
    b, s, d = x.shape
    hd = GQA_HEAD_DIM
    n = w_qkv.shape[1]
    return pl.pallas_call(
        _gqa_qkv_kernel, grid=(b, s // TM),
        in_specs=[pl.BlockSpec((1, TM, d), lambda bi, j: (bi, j, 0)),
                  pl.BlockSpec((1, SUBLANES, d), lambda bi, j: (bi * 2 + jnp.minimum(j, 1), 0, 0)),
                  pl.BlockSpec((4, d), lambda bi, j: (0, 0)),
                  pl.BlockSpec((d, n), lambda bi, j: (0, 0)),
                  pl.BlockSpec((1, hd), lambda bi, j: (0, 0)),
                  pl.BlockSpec((1, hd), lambda bi, j: (0, 0)),
                  pl.BlockSpec((TM, hd), lambda bi, j: (j, 0)),
                  pl.BlockSpec((TM, hd), lambda bi, j: (j, 0))],
        out_specs=[pl.BlockSpec((1, GQA_Q_HEADS, TM, hd), lambda bi, j: (bi, 0, j, 0)),
                   pl.BlockSpec((1, GQA_KV_HEADS, hd, TM), lambda bi, j: (bi, 0, 0, j)),
                   pl.BlockSpec((1, GQA_KV_HEADS, TM, hd), lambda bi, j: (bi, 0, j, 0))],
        out_shape=[jax.ShapeDtypeStruct((b, GQA_Q_HEADS, s, hd), MXU_DTYPE),
                   jax.ShapeDtypeStruct((b, GQA_KV_HEADS, hd, s), MXU_DTYPE),
                   jax.ShapeDtypeStruct((b, GQA_KV_HEADS, s, hd), MXU_DTYPE)],
        compiler_params=_params("arbitrary", "arbitrary"), name="gqa_qkv",
    )(x, modt, ng, w_qkv.astype(MXU_DTYPE), q_gain.reshape(1, hd), k_gain.reshape(1, hd), cos, sin)


def _diff_qkv_kernel(x_ref, mod_ref, ng_ref, w_ref, cos_ref, sin_ref, q_ref, k_ref, v_ref):
    mod = mod_ref[0]
    d = x_ref.shape[2]
    h = _rms(x_ref[0], ng_ref[0:1]) * (1.0 + mod[1:2]) + mod[0:1]
    qkv = _dot(h, w_ref[...])
    cos, sin = cos_ref[...], sin_ref[...]
    hw = 2 * DIFF_HEAD_DIM
    lane = lax.broadcasted_iota(jnp.int32, (x_ref.shape[1], hw), 1)
    quarter = DIFF_HEAD_DIM // 2
    take_up = (lane // quarter) % 2 == 0

    def rope(t):
        rot = jnp.where(take_up, pltpu.roll(t, hw - quarter, 1), pltpu.roll(t, quarter, 1))
        return t * cos + rot * sin

    scale = DIFF_HEAD_DIM ** -0.5 * LOG2E
    first = lane < DIFF_HEAD_DIM
    for i in range(DIFF_HEADS):
        q = rope(qkv[:, i * hw:(i + 1) * hw]) * scale
        q_ref[0, 2 * i] = jnp.where(first, q, 0.0).astype(q_ref.dtype)
        q_ref[0, 2 * i + 1] = jnp.where(first, 0.0, q).astype(q_ref.dtype)
        k_ref[0, i] = rope(qkv[:, d + i * hw:d + (i + 1) * hw]).T.astype(k_ref.dtype)
        v_ref[0, i] = qkv[:, 2 * d + i * hw:2 * d + (i + 1) * hw].astype(v_ref.dtype)


def _diff_qkv(x, modt, ng, w_qkv, cos, sin):
    b, s, d = x.shape
    hw = 2 * DIFF_HEAD_DIM
    n = w_qkv.shape[1]
    return pl.pallas_call(
        _diff_qkv_kernel, grid=(b, s // TM),
        in_specs=[pl.BlockSpec((1, TM, d), lambda bi, j: (bi, j, 0)),
                  pl.BlockSpec((1, SUBLANES, d), lambda bi, j: (bi * 2 + jnp.minimum(j, 1), 0, 0)),
                  pl.BlockSpec((4, d), lambda bi, j: (0, 0)),
                  pl.BlockSpec((d, n), lambda bi, j: (0, 0)),
                  pl.BlockSpec((TM, hw), lambda bi, j: (j, 0)),
                  pl.BlockSpec((TM, hw), lambda bi, j: (j, 0))],
        out_specs=[pl.BlockSpec((1, 2 * DIFF_HEADS, TM, hw), lambda bi, j: (bi, 0, j, 0)),
                   pl.BlockSpec((1, DIFF_HEADS, hw, TM), lambda bi, j: (bi, 0, 0, j)),
                   pl.BlockSpec((1, DIFF_HEADS, TM, hw), lambda bi, j: (bi, 0, j, 0))],
        out_shape=[jax.ShapeDtypeStruct((b, 2 * DIFF_HEADS, s, hw), MXU_DTYPE),
                   jax.ShapeDtypeStruct((b, DIFF_HEADS, hw, s), MXU_DTYPE),
                   jax.ShapeDtypeStruct((b, DIFF_HEADS, s, hw), MXU_DTYPE)],
        compiler_params=_params("arbitrary", "arbitrary"), name="diff_qkv",
    )(x, modt, ng, w_qkv.astype(MXU_DTYPE), cos, sin)


def _flash_kernel(*refs, group, ctx_len, kv_chunk, n_chunks, lam_init):
    if lam_init is None:
        q_ref, kt_ref, v_ref, o_ref, s_even, s_odd = refs
    else:
        q_ref, kt_ref, v_ref, lam_ref, sg_ref, o_ref, s_even, s_odd = refs
    s_bufs = (s_even, s_odd)
    tq, hd = q_ref.shape[2], q_ref.shape[3]
    j = pl.program_id(2)
    rows = group * tq
    q = q_ref[0].reshape(rows, hd)
    n_sub = kv_chunk // SCORE_COLS

    def qk(start):
        return jnp.dot(q, kt_ref[0, 0, :, pl.ds(start, SCORE_COLS)], preferred_element_type=F32)

    def probs(s, m):
        return jnp.exp2((s - m).astype(v_ref.dtype))

    def pv(p, start):
        v = v_ref[0, 0, pl.ds(start, SCORE_COLS), :]
        return jnp.dot(p, jnp.concatenate([v, jnp.ones_like(v)], axis=1), preferred_element_type=F32)

    def fold(x):
        return [x[:, c * LANES:(c + 1) * LANES] for c in range(SCORE_COLS // LANES)]

    def chunk_start(i):
        start = ctx_len + i * kv_chunk
        return start if isinstance(i, int) else pl.multiple_of(start, SCORE_COLS)

    s = qk(0)
    m = jnp.max(s, axis=-1, keepdims=True)
    acc = pv(probs(s, m), 0)

    def latent_keys(state):
        def scores_and_max(i, buf, c, mx):
            sc = qk(chunk_start(i) + c * SCORE_COLS)
            buf[:, c * SCORE_COLS:(c + 1) * SCORE_COLS] = sc
            for part in fold(sc):
                mx = part if mx is None else jnp.maximum(mx, part)
            return mx

        def new_max(m, mx):
            m_new = jnp.maximum(m, jnp.max(mx, axis=-1, keepdims=True))
            return m_new, jnp.exp2(m - m_new)

        def step(i, parity, has_next, m, alpha, acc):
            acc = alpha * acc
            mx = None
            for c in range(n_sub):
                if has_next:
                    mx = scores_and_max(i + 1, s_bufs[1 - parity], c, mx)
                pc = probs(s_bufs[parity][:, c * SCORE_COLS:(c + 1) * SCORE_COLS], m)
                acc = acc + pv(pc, chunk_start(i) + c * SCORE_COLS)
            if has_next:
                m, alpha = new_max(m, mx)
            return m, alpha, acc

        m, acc = state
        mx = None
        for c in range(n_sub):
            mx = scores_and_max(0, s_bufs[0], c, mx)
        m, alpha = new_max(m, mx)
        pairs = (n_chunks - 1) // 2

        def body(t, carry):
            carry = step(2 * t, 0, True, *carry)
            return step(2 * t + 1, 1, True, *carry)

        carry = lax.fori_loop(0, pairs, body, (m, alpha, acc))
        for i in range(2 * pairs, n_chunks):
            carry = step(i, i % 2, i < n_chunks - 1, *carry)
        return carry[0], carry[2]

    _, acc = lax.cond(j > 0, latent_keys, lambda state: state, (m, acc))
    o = acc[:, :hd] / acc[:, hd:]
    if lam_init is None:
        for g in range(group):
            o_ref[0, :, g * hd:(g + 1) * hd] = o[g * tq:(g + 1) * tq].astype(o_ref.dtype)
    else:
        lv = lam_ref[...]
        lam = (jnp.exp(jnp.sum(lv[0:1] * lv[1:2], axis=-1, keepdims=True))
               - jnp.exp(jnp.sum(lv[2:3] * lv[3:4], axis=-1, keepdims=True)) + lam_init)
        od = o[:tq] - lam * o[tq:]
        o_ref[0] = (_rms(od, sg_ref[...]) * (1.0 - lam_init)).astype(o_ref.dtype)


def _flash(q, k, v, group, ctx_len, lam=None, subln=None, lam_init=None):
    b, hq, s, hd = q.shape
    hkv = k.shape[1]
    lat = s - ctx_len
    kv_chunk = min(KV_CHUNK, lat)
    assert ctx_len == SCORE_COLS and kv_chunk % SCORE_COLS == 0 and lat % kv_chunk == 0
    kern =functools.partial(_flash_kernel, group=group, ctx_len=ctx_len, kv_chunk=kv_chunk,
                             n_chunks=lat // kv_chunk, lam_init=lam_init)
    in_specs = [pl.BlockSpec((1, group, TM, hd), lambda bi, h, j: (bi, h, j, 0)),
                pl.BlockSpec((1, 1, hd, s), lambda bi, h, j: (bi, h, 0, 0)),
                pl.BlockSpec((1, 1, s, hd), lambda bi, h, j: (bi, h, 0, 0))]
    args = [q, k, v]
    if lam_init is None:
        out_w = group * hd
    else:
        out_w = hd
        in_specs += [pl.BlockSpec(lam.shape, lambda bi, h, j: (0, 0)),
                     pl.BlockSpec((1, hd), lambda bi, h, j: (0, 0))]
        args += [lam, subln.reshape(1, hd)]
    return pl.pallas_call(
        kern, grid=(b, hkv, s // TM), in_specs=in_specs,
        out_specs=pl.BlockSpec((1, TM, out_w), lambda bi, h, j: (bi, j, h)),
        out_shape=jax.ShapeDtypeStruct((b, s, hkv * out_w), MXU_DTYPE),
        scratch_shapes=[pltpu.VMEM((group * TM, kv_chunk), F32), pltpu.VMEM((group * TM, kv_chunk), F32)],
        compiler_params=_params("arbitrary", "arbitrary", "arbitrary"), name="flash_attention",
    )(*args)


def _attn_out_body(xt, mod, ng, o_ref, wo_ref):
    return _dot(o_ref[0], wo_ref[...])


def _attn_out_layer(x, modt, ng, router, o, w_o):
    b, s, d = x.shape
    specs = [pl.BlockSpec((1, TM, d), lambda bi, j: (bi, j, 0)),
             pl.BlockSpec((d, d), lambda bi, j: (0, 0))]
    return _mixer_call(_attn_out_body, x, modt, ng, router, (o, w_o.astype(MXU_DTYPE)), specs, "attn_out")


def _rope_tables(ctx_len, lat_len, dim, reps):
    n = dim // 4
    pos = jnp.arange(lat_len, dtype=jnp.int32)
    row = (pos // GRID_W).astype(F32)
    col = (pos % GRID_W).astype(F32)
    freqs = ROPE_THETA ** (-jnp.arange(n, dtype=F32) / n)
    ang = jnp.concatenate([row[:, None] * freqs, col[:, None] * freqs], axis=-1)
    ang = jnp.concatenate([jnp.zeros((ctx_len, dim // 2), F32), ang], axis=0)
    cos, sin = jnp.cos(ang), jnp.sin(ang)
    return jnp.tile(cos, (1, 2 * reps)), jnp.tile(jnp.concatenate([-sin, sin], axis=-1), (1, reps))


def _dispatch_kernel(tail_ref, dest_ref, h_ref, hs_hbm, zero_buf, sem):
    @pl.when(pl.program_id(0) == 0)
    def _():
        zero_buf[...] = jnp.zeros_like(zero_buf)

        def tail_copy(e):
            start = pl.multiple_of(tail_ref[e], SUBLANES)
            return pltpu.make_async_copy(zero_buf, hs_hbm.at[pl.ds(start, EXPERT_ROWS)], sem)

        for e in range(N_EXPERTS):
            tail_copy(e).start()
        for e in range(N_EXPERTS):
            tail_copy(e).wait()

    def row_copy(src_row, dst_row):
        return pltpu.make_async_copy(h_ref.at[pl.ds(src_row, 1)], hs_hbm.at[pl.ds(dst_row, 1)], sem)

    def issue(r, c):
        for k in range(TOP_K):
            row_copy(r, dest_ref[0, 0, r * TOP_K + k]).start()
        return c

    def drain(r, c):
        for k in range(TOP_K):
            row_copy(r, dest_ref[0, 0, r * TOP_K + k]).wait()
        return c

    lax.fori_loop(0, TM, issue, 0)
    lax.fori_loop(0, TM, drain, 0)


def _dispatch(h2, dest3, tail_start, n_pad):
    t, d = h2.shape
    grid_spec = pltpu.PrefetchScalarGridSpec(
        num_scalar_prefetch=1, grid=(t // TM,),
        in_specs=[pl.BlockSpec((1, 1, TM * TOP_K), lambda i, tail: (i, 0, 0), memory_space=pltpu.SMEM),
                  pl.BlockSpec((TM, d), lambda i, tail: (i, 0))],
        out_specs=pl.BlockSpec(memory_space=pl.ANY),
        scratch_shapes=[pltpu.VMEM((EXPERT_ROWS, d), F32), pltpu.SemaphoreType.DMA])
    return pl.pallas_call(
        _dispatch_kernel, grid_spec=grid_spec,
        out_shape=jax.ShapeDtypeStruct((n_pad, d), F32),
        compiler_params=pltpu.CompilerParams(dimension_semantics=("arbitrary",), has_side_effects=True),
        name="moe_dispatch",
    )(tail_start, dest3, h2)


def _expert_kernel(be_ref, nused_ref, hs_ref, w1_ref, b1_ref, w2_ref, b2_ref, o_ref):
    del be_ref
    i = pl.program_id(0)
    f = w2_ref.shape[1]

    @pl.when(i < nused_ref[0])
    def _():
        gu = _dot(hs_ref[...], w1_ref[0]) + b1_ref[0]
        g = jnp.minimum(gu[:, :f], SWIGLU_LIMIT)
        u = jnp.clip(gu[:, f:], -SWIGLU_LIMIT, SWIGLU_LIMIT)
        act = g * _sigmoid(SWIGLU_ALPHA * g) * (u + 1.0)
        o_ref[...] = _dot(act, w2_ref[0]) + b2_ref[0]

    @pl.when(i >= nused_ref[0])
    def _():
        o_ref[...] = jnp.zeros_like(o_ref)


def _experts(hs, block_e, n_used, w1, b1, w2, b2):
    n_pad, d = hs.shape
    e, _, f2 = w1.shape
    f = w2.shape[1]
    tb = EXPERT_ROWS
    grid_spec = pltpu.PrefetchScalarGridSpec(
        num_scalar_prefetch=2, grid=(n_pad // tb,),
        in_specs=[pl.BlockSpec((tb, d), lambda i, be, nu: (i, 0)),
                  pl.BlockSpec((1, d, f2), lambda i, be, nu: (be[i], 0, 0)),
                  pl.BlockSpec((1, 1, f2), lambda i, be, nu: (be[i], 0, 0)),
                  pl.BlockSpec((1, f, d), lambda i, be, nu: (be[i], 0, 0)),
                  pl.BlockSpec((1, 1, d), lambda i, be, nu: (be[i], 0, 0))],
        out_specs=pl.BlockSpec((tb, d), lambda i, be, nu: (i, 0)))
    return pl.pallas_call(
        _expert_kernel, grid_spec=grid_spec,
        out_shape=jax.ShapeDtypeStruct((n_pad, d), F32),
        compiler_params=_params("arbitrary"), name="moe_experts",
    )(block_e, n_used, hs, w1, b1.reshape(e, 1, f2), w2, b2.reshape(e, 1, d))


def _combine_kernel(dest_ref, ys_hbm, mf_ref, x1_ref, mod_ref, ng_ref, x2_ref, buf, sem):
    def row_copy(src_row, k, r):
        return pltpu.make_async_copy(ys_hbm.at[pl.ds(src_row, 1)], buf.at[k, pl.ds(r, 1)], sem)

    def issue(r, c):
        for k in range(TOP_K):
            row_copy(dest_ref[0, 0, r * TOP_K + k], k, r).start()
        return c

    def drain(r, c):
        for k in range(TOP_K):
            row_copy(dest_ref[0, 0, r * TOP_K + k], k, r).wait()
        return c

    lax.fori_loop(0, TM, issue, 0)
    lax.fori_loop(0, TM, drain, 0)
    gates = mf_ref[...]
    y = gates[:, 0:1] * buf[0]
    for k in range(1, TOP_K):
        y = y + gates[:, k:k + 1] * buf[k]
    mod = mod_ref[0]
    x2_ref[0] = x1_ref[0] + mod[5:6] * _rms(y, ng_ref[3:4])


def _combine(ys, dest3, mf, x1, modt, ng, latent_only):
    b, s, d = x1.shape
    nt = s // TM
    if latent_only:
        out_spec = pl.BlockSpec((1, TM, d), lambda bi, j: (bi, jnp.maximum(j - 1, 0), 0))
        out_shape = jax.ShapeDtypeStruct((b, s - TM, d), F32)
    else:
        out_spec = pl.BlockSpec((1, TM, d), lambda bi, j: (bi, j, 0))
        out_shape = jax.ShapeDtypeStruct((b, s, d), F32)
    return pl.pallas_call(
        _combine_kernel, grid=(b, nt),
        in_specs=[pl.BlockSpec((1, 1, TM * TOP_K), lambda bi, j: (bi * nt + j, 0, 0), memory_space=pltpu.SMEM),
                  pl.BlockSpec(memory_space=pl.ANY),
                  pl.BlockSpec((TM, LANES), lambda bi, j: (bi * nt + j, 0)),
                  pl.BlockSpec((1, TM, d), lambda bi, j: (bi, j, 0)),
                  pl.BlockSpec((1, SUBLANES, d), lambda bi, j: (bi * 2 + jnp.minimum(j, 1), 0, 0)),
                  pl.BlockSpec((4, d), lambda bi, j: (0, 0))],
        out_specs=out_spec, out_shape=out_shape,
        scratch_shapes=[pltpu.VMEM((TOP_K, TM, d), F32), pltpu.SemaphoreType.DMA],
        compiler_params=_params("arbitrary", "arbitrary"), name="moe_combine",
    )(dest3, ys, mf, x1, modt, ng)


def _moe(x1, h2, mi, mf, cnt, modt, ng, w1, b1, w2, b2, latent_only):
    t = h2.shape[0]
    tb = EXPERT_ROWS
    counts = cnt[0, :N_EXPERTS].astype(jnp.int32)
    padded = (counts + tb - 1) // tb * tb
    pend = jnp.cumsum(padded)
    pstart = pend - padded
    dest = pstart[mi[:, :TOP_K]] + mi[:, TOP_K:2 * TOP_K]
    n_blocks = -(-(t * TOP_K) // tb) + N_EXPERTS
    block_row0 = jnp.arange(n_blocks, dtype=jnp.int32) * tb
    block_e = jnp.minimum(jnp.sum((pend[None, :] <= block_row0[:, None]).astype(jnp.int32), axis=1),
                          N_EXPERTS - 1)
    n_used = (pend[-1:] // tb).astype(jnp.int32)
    dest3 = dest.reshape(t // TM, 1, TM * TOP_K)
    tail_start = jnp.maximum(pend - tb, 0).astype(jnp.int32)
    hs = _dispatch(h2, dest3, tail_start, n_blocks * tb)
    ys = _experts(hs, block_e, n_used, w1, b1, w2, b2)
    return _combine(ys, dest3, mf, x1, modt, ng, latent_only)


def kernel(x, c, ctx, c_ctx, mod_w, mod_b, norm_g, pool_w, pool_scale, gqa_w_qkv, gqa_w_o, gqa_q_gain, gqa_k_gain, conv_w_in, conv_w_dw, conv_w_out, diff_w_qkv, diff_w_o, diff_lambda, diff_subln_gain, router_w, router_b, moe_w1, moe_b1, moe_w2, moe_b2):
    b, lat_len, d = x.shape
    ctx_len = ctx.shape[1]
    depth = mod_w.shape[0]
    n_mixers = 4
    assert ctx_len == TM and lat_len % TM == 0 and b + 1 <= SUBLANES and d % LANES == 0

    xa = jnp.concatenate([ctx, x], axis=1)

    cv = jnp.zeros((SUBLANES, d), F32).at[:b].set(c).at[b].set(c_ctx)
    mods = _modulation(cv, mod_w, mod_b).reshape(depth, SUBLANES, MOD_CHUNKS, d)
    modt_all = jnp.stack([jnp.broadcast_to(mods[:, b][:, None], (depth, b, MOD_CHUNKS, d)), mods[:, :b]], axis=2)
    modt_all = jnp.pad(modt_all, ((0, 0), (0, 0), (0, 0), (0, SUBLANES - MOD_CHUNKS), (0, 0)))
    modt_all = modt_all.reshape(depth, b * 2, SUBLANES, d)

    cos_b, sin_b = _rope_tables(ctx_len, lat_len, GQA_HEAD_DIM, 1)
    cos_d, sin_d = _rope_tables(ctx_len, lat_len, DIFF_HEAD_DIM, 2)

    for i in range(depth):
        m, jj = i % n_mixers, i // n_mixers
        modt, ng = modt_all[i], norm_g[i]
        rw = jnp.zeros((d, LANES), F32).at[:, :N_EXPERTS].set(router_w[i])
        rwh = rw.astype(jnp.bfloat16)
        rwl = (rw - rwh.astype(F32)).astype(jnp.bfloat16)
        rb = jnp.full((1, LANES), NEG_BIG, F32).at[0, :N_EXPERTS].set(router_b[i])
        router = (rwh, rwl, rb)
        if m == 0:
            res = _pool_layer(xa, modt, ng, router, pool_w[jj], pool_scale[jj], ctx_len)
        elif m == 1:
            q, k, v = _gqa_qkv(xa, modt, ng, gqa_w_qkv[jj], gqa_q_gain[jj], gqa_k_gain[jj], cos_b, sin_b)
            o = _flash(q, k, v, GQA_Q_HEADS // GQA_KV_HEADS, ctx_len)
            res = _attn_out_layer(xa, modt, ng, router, o, gqa_w_o[jj])
        elif m == 2:
            res = _conv_layer(xa, modt, ng, router, conv_w_in[jj], conv_w_dw[jj], conv_w_out[jj])
        else:
            lam_init = 0.8 - 0.6 * math.exp(-0.3 * i)
            q, k, v = _diff_qkv(xa, modt, ng, diff_w_qkv[jj], cos_d, sin_d)
            o = _flash(q, k, v, 2, ctx_len, lam=diff_lambda[jj], subln=diff_subln_gain[jj], lam_init=lam_init)
            res = _attn_out_layer(xa, modt, ng, router, o, diff_w_o[jj])
        x1, h2, mi, mf, cnt = res
        xa = _moe(x1, h2, mi, mf, cnt, modt, ng,
                  moe_w1[i].astype(MXU_DTYPE), moe_b1[i], moe_w2[i].astype(MXU_DTYPE), moe_b2[i],
                  latent_only=(i == depth - 1))
    return xa
```

```python
import functools
import math

import jax
import jax.numpy as jnp
from jax import lax
from jax.experimental import pallas as pl
from jax.experimental.pallas import tpu as pltpu

RMS_EPS = 1e-6
ROPE_THETA = 10000.0
GRID_W = 64
POOL_WINDOWS = (2, 4, 8, 16)
N_EXPERTS = 32
TOP_K = 4
SWIGLU_LIMIT = 7.0
SWIGLU_ALPHA = 1.702
GQA_HEAD_DIM = 128
GQA_Q_HEADS = 8
GQA_KV_HEADS = 2
DIFF_HEAD_DIM = 64
DIFF_HEADS = 8
MOD_CHUNKS = 6

LANES = 128
SUBLANES = 8
VMEM_LIMIT_BYTES = 56 * 1024 * 1024

TM = 256
EXPERT_ROWS = 512
KV_CHUNK = 1024
SCORE_COLS = 256
LOG2E = math.log2(math.e)
HALO = SUBLANES

MXU_DTYPE = jnp.bfloat16
F32 = jnp.float32
NEG_BIG = -1e30


def _params(*sem):
    return pltpu.CompilerParams(dimension_semantics=sem, vmem_limit_bytes=VMEM_LIMIT_BYTES)


def _dot(a, b):
    return jnp.dot(a.astype(MXU_DTYPE), b.astype(MXU_DTYPE), preferred_element_type=F32)


def _rms(x, g):
    return x * lax.rsqrt(jnp.mean(x * x, axis=-1, keepdims=True) + RMS_EPS) * g


def _sigmoid(z):
    return 1.0 / (1.0 + jnp.exp(-z))


def _shift_up(a, s):
    n = a.shape[0]
    return pltpu.roll(a, (n - s) % n, 0)


def _mod_kernel(cv_ref, w_ref, b_ref, o_ref):
    a = cv_ref[...]
    o_ref[0] = _dot(a * _sigmoid(a), w_ref[0]) + b_ref[0]


def _modulation(cv, mod_w, mod_b):
    depth, d, n = mod_w.shape
    tn = n // 4
    return pl.pallas_call(
        _mod_kernel,
        grid=(depth, n // tn),
        in_specs=[pl.BlockSpec((SUBLANES, d), lambda i, k: (0, 0)),
                  pl.BlockSpec((1, d, tn), lambda i, k: (i, 0, k)),
                  pl.BlockSpec((1, 1, tn), lambda i, k: (i, 0, k))],
        out_specs=pl.BlockSpec((1, SUBLANES, tn), lambda i, k: (i, 0, k)),
        out_shape=jax.ShapeDtypeStruct((depth, SUBLANES, n), F32),
        compiler_params=_params("arbitrary", "arbitrary"),
        name="modulation",
    )(cv, mod_w, mod_b.reshape(depth, 1, n))


def _post_mixer(y, x, mod, ng, rwh_ref, rwl_ref, rb_ref, base_ref,
                x1_ref, h2_ref, mi_ref, mf_ref, cnt_ref):
    tm = x.shape[0]
    x1 = x + mod[2:3] * _rms(y, ng[1:2])
    h2 = _rms(x1, ng[2:3]) * (1.0 + mod[4:5]) + mod[3:4]
    x1_ref[0] = x1
    h2_ref[...] = h2

    h_hi = h2.astype(jnp.bfloat16)
    h_lo = (h2 - h_hi.astype(F32)).astype(jnp.bfloat16)
    rwh = rwh_ref[...]
    logits = (jnp.dot(h_hi, rwh, preferred_element_type=F32)
              + jnp.dot(h_lo, rwh, preferred_element_type=F32)
              + jnp.dot(h_hi, rwl_ref[...], preferred_element_type=F32)) + rb_ref[...]

    lane = lax.broadcasted_iota(jnp.int32, (tm, LANES), 1)
    lane_f = lane.astype(F32)
    vals, idxs, hots = [], [], []
    l = logits
    for _ in range(TOP_K):
        m = jnp.max(l, axis=-1, keepdims=True)
        idx = jnp.min(jnp.where(l == m, lane_f, float(LANES)), axis=-1, keepdims=True)
        hot = lane_f == idx
        vals.append(m)
        idxs.append(idx)
        hots.append(hot)
        l = jnp.where(hot, -3e38, l)
    exps = [jnp.exp(v - vals[0]) for v in vals]
    den = exps[0] + exps[1] + exps[2] + exps[3]

    @pl.when((pl.program_id(0) == 0) & (pl.program_id(1) == 0))
    def _():
        base_ref[...] = jnp.zeros_like(base_ref)

    hot_all = (hots[0].astype(F32) + hots[1].astype(F32) + hots[2].astype(F32) + hots[3].astype(F32))
    rows = lax.broadcasted_iota(jnp.int32, (tm, tm), 0)
    cols = lax.broadcasted_iota(jnp.int32, (tm, tm), 1)
    lower = (rows > cols).astype(jnp.bfloat16)
    before = jnp.dot(lower, hot_all.astype(jnp.bfloat16), preferred_element_type=F32) + base_ref[...]
    mi = jnp.zeros((tm, LANES), F32)
    mf = jnp.zeros((tm, LANES), F32)
    for k in range(TOP_K):
        rank = jnp.sum(jnp.where(hots[k], before, 0.0), axis=-1, keepdims=True)
        mi = mi + jnp.where(lane == k, idxs[k], 0.0) + jnp.where(lane == TOP_K + k, rank, 0.0)
        mf = mf + jnp.where(lane == k, exps[k] / den, 0.0)
    mi_ref[...] = mi.astype(jnp.int32)
    mf_ref[...] = mf
    base_ref[...] = base_ref[...] + jnp.sum(hot_all, axis=0, keepdims=True)
    cnt_ref[...] = base_ref[...]


def _mixer_call(body, x, modt, ng, router, extra_args, extra_specs, name):
    b, s, d = x.shape
    nt = s // TM
    rwh, rwl, rb = router
    n_extra = len(extra_args)

    def kern(*refs):
        x_ref, mod_ref, ng_ref, rwh_ref, rwl_ref, rb_ref = refs[:6]
        extra = refs[6:6 + n_extra]
        x1_ref, h2_ref, mi_ref, mf_ref, cnt_ref, base_ref = refs[6 + n_extra:]
        xt = x_ref[0]
        mod = mod_ref[0]
        ngv = ng_ref[...]
        y = body(xt, mod, ngv, *extra)
        _post_mixer(y, xt, mod, ngv, rwh_ref, rwl_ref, rb_ref, base_ref,
                    x1_ref, h2_ref, mi_ref, mf_ref, cnt_ref)

    in_specs = [
        pl.BlockSpec((1, TM, d), lambda bi, j: (bi, j, 0)),
        pl.BlockSpec((1, SUBLANES, d), lambda bi, j: (bi * 2 + jnp.minimum(j, 1), 0, 0)),
        pl.BlockSpec((4, d), lambda bi, j: (0, 0)),
        pl.BlockSpec((d, LANES), lambda bi, j: (0, 0)),
        pl.BlockSpec((d, LANES), lambda bi, j: (0, 0)),
        pl.BlockSpec((1, LANES), lambda bi, j: (0, 0)),
    ] + list(extra_specs)
    out_specs = [
        pl.BlockSpec((1, TM, d), lambda bi, j: (bi, j, 0)),
        pl.BlockSpec((TM, d), lambda bi, j: (bi * nt + j, 0)),
        pl.BlockSpec((TM, LANES), lambda bi, j: (bi * nt + j, 0)),
        pl.BlockSpec((TM, LANES), lambda bi, j: (bi * nt + j, 0)),
        pl.BlockSpec((1, LANES), lambda bi, j: (0, 0)),
    ]
    out_shape = [
        jax.ShapeDtypeStruct((b, s, d), F32),
        jax.ShapeDtypeStruct((b * s, d), F32),
        jax.ShapeDtypeStruct((b * s, LANES), jnp.int32),
        jax.ShapeDtypeStruct((b * s, LANES), F32),
        jax.ShapeDtypeStruct((1, LANES), F32),
    ]
    return pl.pallas_call(
        kern, grid=(b, nt), in_specs=in_specs, out_specs=out_specs, out_shape=out_shape,
        scratch_shapes=[pltpu.VMEM((1, LANES), F32)],
        compiler_params=_params("arbitrary", "arbitrary"), name=name,
    )(x, modt, ng, rwh, rwl, rb, *extra_args)


def _halo_specs(s, d):
    last = s // HALO - 1
    per_tile = TM // HALO
    prev = pl.BlockSpec((1, HALO, d), lambda bi, j: (bi, jnp.maximum(j * per_tile - 1, 0), 0))
    nxt = pl.BlockSpec((1, HALO, d), lambda bi, j: (bi, jnp.minimum((j + 1) * per_tile, last), 0))
    return prev, nxt


def _halo_valid():
    j = pl.program_id(1)
    nt = pl.num_programs(1)
    return j >= 2, (j >= 1) & (j <= nt - 2)


def _pool_body(ctx_len, lat_len, xt, mod, ng, xp_ref, xn_ref, pw_ref, ps_ref):
    tm, d = xt.shape
    group = d // len(POOL_WINDOWS)
    prev_ok, next_ok = _halo_valid()
    pre = lambda rows: _rms(rows, ng[0:1]) * (1.0 + mod[1:2]) + mod[0:1]
    hc = pre(xt)
    hp = jnp.where(prev_ok, pre(xp_ref[0]), 0.0)
    hn = jnp.where(next_ok, pre(xn_ref[0]), 0.0)
    ext = jnp.concatenate([hp, hc, hn], axis=0)

    j = pl.program_id(1)
    pos = lax.broadcasted_iota(jnp.int32, (tm, 1), 0) + jnp.where(j == 0, 0, (j - 1) * tm)
    seq_len = jnp.where(j == 0, ctx_len, lat_len)
    outs = []
    for g, w in enumerate(POOL_WINDOWS):
        e = ext[:, g * group:(g + 1) * group]
        run, span = e, 1
        while span < w:
            run = run + _shift_up(run, span)
            span *= 2
        win = _shift_up(run, HALO - w // 2)[:tm] if w // 2 != HALO else run[:tm]
        lo = jnp.maximum(pos - w // 2, 0)
        hi = jnp.minimum(pos + (w - w // 2), seq_len)
        diff = win / (hi - lo).astype(F32) - hc[:, g * group:(g + 1) * group]
        outs.append(_dot(diff, pw_ref[g]))
    return jnp.concatenate(outs, axis=1) * ps_ref[...]


def _pool_layer(x, modt, ng, router, pool_w, pool_scale, ctx_len):
    b, s, d = x.shape
    prev, nxt = _halo_specs(s, d)
    g = len(POOL_WINDOWS)
    specs = [prev, nxt,
             pl.BlockSpec((g, d // g, d // g), lambda bi, j: (0, 0, 0)),
             pl.BlockSpec((1, d), lambda bi, j: (0, 0))]
    body = functools.partial(_pool_body, ctx_len, s - ctx_len)
    return _mixer_call(body, x, modt, ng, router,
                       (x, x, pool_w.astype(MXU_DTYPE), pool_scale.reshape(1, d)), specs, "pool_mixer")


def _conv_body(xt, mod, ng, xp_ref, xn_ref, win_ref, wdw_ref, wout_ref):
    tm, d = xt.shape
    prev_ok, next_ok = _halo_valid()
    rows = jnp.concatenate([xp_ref[0], xt, xn_ref[0]], axis=0)
    h = _rms(rows, ng[0:1]) * (1.0 + mod[1:2]) + mod[0:1]
    proj = _dot(h, win_ref[...])
    gate_b = proj[HALO:HALO + tm, :d]
    u = proj[:, d:2 * d] * proj[:, 2 * d:]
    r = lax.broadcasted_iota(jnp.int32, (tm + 2 * HALO, 1), 0)
    keep = ((r >= HALO) | prev_ok) & ((r < HALO + tm) | next_ok)
    u = jnp.where(keep, u, 0.0)
    wdw = wdw_ref[...]
    conv = (wdw[0:1] * _shift_up(u, HALO - 1)[:tm] + wdw[1:2] * u[HALO:HALO + tm]
            + wdw[2:3] * _shift_up(u, HALO + 1)[:tm])
    return _dot(gate_b * conv, wout_ref[...])


def _conv_layer(x, modt, ng, router, w_in, w_dw, w_out):
    b, s, d = x.shape
    prev, nxt = _halo_specs(s, d)
    specs = [prev, nxt,
             pl.BlockSpec((d, 3 * d), lambda bi, j: (0, 0)),
             pl.BlockSpec((SUBLANES, d), lambda bi, j: (0, 0)),
             pl.BlockSpec((d, d), lambda bi, j: (0, 0))]
    w_dw8 = jnp.zeros((SUBLANES, d), F32).at[:w_dw.shape[0]].set(w_dw)
    return _mixer_call(_conv_body, x, modt, ng, router,
                       (x, x, w_in.astype(MXU_DTYPE), w_dw8, w_out.astype(MXU_DTYPE)), specs, "conv_mixer")


def _gqa_qkv_kernel(x_ref, mod_ref, ng_ref, w_ref, qg_ref, kg_ref, cos_ref, sin_ref,
                    q_ref, k_ref, v_ref):
    mod = mod_ref[0]
    h = _rms(x_ref[0], ng_ref[0:1]) * (1.0 + mod[1:2]) + mod[0:1]
    qkv = _dot(h, w_ref[...])
    cos, sin = cos_ref[...], sin_ref[...]
    hd = GQA_HEAD_DIM

    def rope(t):
        return t * cos + pltpu.roll(t, hd // 2, 1) * sin

    scale = hd ** -0.5 * LOG2E
    for i in range(GQA_Q_HEADS):
        q = rope(_rms(qkv[:, i * hd:(i + 1) * hd], qg_ref[...]))
        q_ref[0, i] = (q * scale).astype(q_ref.dtype)
    for i in range(GQA_KV_HEADS):
        o = (GQA_Q_HEADS + i) * hd
        k_ref[0, i] = rope(_rms(qkv[:, o:o + hd], kg_ref[...])).T.astype(k_ref.dtype)
        o = (GQA_Q_HEADS + GQA_KV_HEADS + i) * hd
        v_ref[0, i] = qkv[:, o:o + hd].astype(v_ref.dtype)


def _gqa_qkv(x, modt, ng, w_qkv, q_gain, k_gain, cos, sin):
    b, s, d = x.shape
    hd = GQA_HEAD_DIM
    n = w_qkv.shape[1]
    return pl.pallas_call(
        _gqa_qkv_kernel, grid=(b, s // TM),
        in_specs=[pl.BlockSpec((1, TM, d), lambda bi, j: (bi, j, 0)),
                  pl.BlockSpec((1, SUBLANES, d), lambda bi, j: (bi * 2 + jnp.minimum(j, 1), 0, 0)),
                  pl.BlockSpec((4, d), lambda bi, j: (0, 0)),
                  pl.BlockSpec((d, n), lambda bi, j: (0, 0)),
                  pl.BlockSpec((1, hd), lambda bi, j: (0, 0)),
                  pl.BlockSpec((1, hd), lambda bi, j: (0, 0)),
                  pl.BlockSpec((TM, hd), lambda bi, j: (j, 0)),
                  pl.BlockSpec((TM, hd), lambda bi, j: (j, 0))],
        out_specs=[pl.BlockSpec((1, GQA_Q_HEADS, TM, hd), lambda bi, j: (bi, 0, j, 0)),
                   pl.BlockSpec((1, GQA_KV_HEADS, hd, TM), lambda bi, j: (bi, 0, 0, j)),
                   pl.BlockSpec((1, GQA_KV_HEADS, TM, hd), lambda bi, j: (bi, 0, j, 0))],
        out_shape=[jax.ShapeDtypeStruct((b, GQA_Q_HEADS, s, hd), MXU_DTYPE),
                   jax.ShapeDtypeStruct((b, GQA_KV_HEADS, hd, s), MXU_DTYPE),
                   jax.ShapeDtypeStruct((b, GQA_KV_HEADS, s, hd), MXU_DTYPE)],
        compiler_params=_params("arbitrary", "arbitrary"), name="gqa_qkv",
    )(x, modt, ng, w_qkv.astype(MXU_DTYPE), q_gain.reshape(1, hd), k_gain.reshape(1, hd), cos, sin)


def _diff_qkv_kernel(x_ref, mod_ref, ng_ref, w_ref, cos_ref, sin_ref, q_ref, k_ref, v_ref):
    mod = mod_ref[0]
    d = x_ref.shape[2]
    h = _rms(x_ref[0], ng_ref[0:1]) * (1.0 + mod[1:2]) + mod[0:1]
    qkv = _dot(h, w_ref[...])
    cos, sin = cos_ref[...], sin_ref[...]
    hw = 2 * DIFF_HEAD_DIM
    lane = lax.broadcasted_iota(jnp.int32, (x_ref.shape[1], hw), 1)
    quarter = DIFF_HEAD_DIM // 2
    take_up = (lane // quarter) % 2 == 0

    def rope(t):
        rot = jnp.where(take_up, pltpu.roll(t, hw - quarter, 1), pltpu.roll(t, quarter, 1))
        return t * cos + rot * sin

    scale = DIFF_HEAD_DIM ** -0.5 * LOG2E
    first = lane < DIFF_HEAD_DIM
    for i in range(DIFF_HEADS):
        q = rope(qkv[:, i * hw:(i + 1) * hw]) * scale
        q_ref[0, 2 * i] = jnp.where(first, q, 0.0).astype(q_ref.dtype)
        q_ref[0, 2 * i + 1] = jnp.where(first, 0.0, q).astype(q_ref.dtype)
        k_ref[0, i] = rope(qkv[:, d + i * hw:d + (i + 1) * hw]).T.astype(k_ref.dtype)
        v_ref[0, i] = qkv[:, 2 * d + i * hw:2 * d + (i + 1) * hw].astype(v_ref.dtype)


def _diff_qkv(x, modt, ng, w_qkv, cos, sin):
    b, s, d = x.shape
    hw = 2 * DIFF_HEAD_DIM
    n = w_qkv.shape[1]
    return pl.pallas_call(
        _diff_qkv_kernel, grid=(b, s // TM),
        in_specs=[pl.BlockSpec((1, TM, d), lambda bi, j: (bi, j, 0)),
                  pl.BlockSpec((1, SUBLANES, d), lambda bi, j: (bi * 2 + jnp.minimum(j, 1), 0, 0)),
                  pl.BlockSpec((4, d), lambda bi, j: (0, 0)),
                  pl.BlockSpec((d, n), lambda bi, j: (0, 0)),
                  pl.BlockSpec((TM, hw), lambda bi, j: (j, 0)),
                  pl.BlockSpec((TM, hw), lambda bi, j: (j, 0))],
        out_specs=[pl.BlockSpec((1, 2 * DIFF_HEADS, TM, hw), lambda bi, j: (bi, 0, j, 0)),
                   pl.BlockSpec((1, DIFF_HEADS, hw, TM), lambda bi, j: (bi, 0, 0, j)),
                   pl.BlockSpec((1, DIFF_HEADS, TM, hw), lambda bi, j: (bi, 0, j, 0))],
        out_shape=[jax.ShapeDtypeStruct((b, 2 * DIFF_HEADS, s, hw), MXU_DTYPE),
                   jax.ShapeDtypeStruct((b, DIFF_HEADS, hw, s), MXU_DTYPE),
                   jax.ShapeDtypeStruct((b, DIFF_HEADS, s, hw), MXU_DTYPE)],
        compiler_params=_params("arbitrary", "arbitrary"), name="diff_qkv",
    )(x, modt, ng, w_qkv.astype(MXU_DTYPE), cos, sin)


def _flash_kernel(*refs, group, ctx_len, kv_chunk, n_chunks, lam_init):
    if lam_init is None:
        q_ref, kt_ref, v_ref, o_ref, s_even, s_odd = refs
    else:
        q_ref, kt_ref, v_ref, lam_ref, sg_ref, o_ref, s_even, s_odd = refs
    s_bufs = (s_even, s_odd)
    tq, hd = q_ref.shape[2], q_ref.shape[3]
    j = pl.program_id(2)
    rows = group * tq
    q = q_ref[0].reshape(rows, hd)
    n_sub = kv_chunk // SCORE_COLS

    def qk(start):
        return jnp.dot(q, kt_ref[0, 0, :, pl.ds(start, SCORE_COLS)], preferred_element_type=F32)

    def probs(s, m):
        return jnp.exp2((s - m).astype(v_ref.dtype))

    def pv(p, start):
        v = v_ref[0, 0, pl.ds(start, SCORE_COLS), :]
        return jnp.dot(p, jnp.concatenate([v, jnp.ones_like(v)], axis=1), preferred_element_type=F32)

    def fold(x):
        return [x[:, c * LANES:(c + 1) * LANES] for c in range(SCORE_COLS // LANES)]

    def chunk_start(i):
        start = ctx_len + i * kv_chunk
        return start if isinstance(i, int) else pl.multiple_of(start, SCORE_COLS)

    s = qk(0)
    m = jnp.max(s, axis=-1, keepdims=True)
    acc = pv(probs(s, m), 0)

    def latent_keys(state):
        def scores_and_max(i, buf, c, mx):
            sc = qk(chunk_start(i) + c * SCORE_COLS)
            buf[:, c * SCORE_COLS:(c + 1) * SCORE_COLS] = sc
            for part in fold(sc):
                mx = part if mx is None else jnp.maximum(mx, part)
            return mx

        def new_max(m, mx):
            m_new = jnp.maximum(m, jnp.max(mx, axis=-1, keepdims=True))
            return m_new, jnp.exp2(m - m_new)

        def step(i, parity, has_next, m, alpha, acc):
            acc = alpha * acc
            mx = None
            for c in range(n_sub):
                if has_next:
                    mx = scores_and_max(i + 1, s_bufs[1 - parity], c, mx)
                pc = probs(s_bufs[parity][:, c * SCORE_COLS:(c + 1) * SCORE_COLS], m)
                acc = acc + pv(pc, chunk_start(i) + c * SCORE_COLS)
            if has_next:
                m, alpha = new_max(m, mx)
            return m, alpha, acc

        m, acc = state
        mx = None
        for c in range(n_sub):
            mx = scores_and_max(0, s_bufs[0], c, mx)
        m, alpha = new_max(m, mx)
        pairs = (n_chunks - 1) // 2

        def body(t, carry):
            carry = step(2 * t, 0, True, *carry)
            return step(2 * t + 1, 1, True, *carry)

        carry = lax.fori_loop(0, pairs, body, (m, alpha, acc))
        for i in range(2 * pairs, n_chunks):
            carry = step(i, i % 2, i < n_chunks - 1, *carry)
        return carry[0], carry[2]

    _, acc = lax.cond(j > 0, latent_keys, lambda state: state, (m, acc))
    o = acc[:, :hd] / acc[:, hd:]
    if lam_init is None:
        for g in range(group):
            o_ref[0, :, g * hd:(g + 1) * hd] = o[g * tq:(g + 1) * tq].astype(o_ref.dtype)
    else:
        lv = lam_ref[...]
        lam = (jnp.exp(jnp.sum(lv[0:1] * lv[1:2], axis=-1, keepdims=True))
               - jnp.exp(jnp.sum(lv[2:3] * lv[3:4], axis=-1, keepdims=True)) + lam_init)
        od = o[:tq] - lam * o[tq:]
        o_ref[0] = (_rms(od, sg_ref[...]) * (1.0 - lam_init)).astype(o_ref.dtype)


def _flash(q, k, v, group, ctx_len, lam=None, subln=None, lam_init=None):
    b, hq, s, hd = q.shape
    hkv = k.shape[1]
    lat = s - ctx_len
    kv_chunk = min(KV_CHUNK, lat)
    assert ctx_len == SCORE_COLS and kv_chunk % SCORE_COLS == 0 and lat % kv_chunk == 0
    kern =functools.partial(_flash_kernel, group=group, ctx_len=ctx_len, kv_chunk=kv_chunk,
                             n_chunks=lat // kv_chunk, lam_init=lam_init)
    in_specs = [pl.BlockSpec((1, group, TM, hd), lambda bi, h, j: (bi, h, j, 0)),
                pl.BlockSpec((1, 1, hd, s), lambda bi, h, j: (bi, h, 0, 0)),
                pl.BlockSpec((1, 1, s, hd), lambda bi, h, j: (bi, h, 0, 0))]
    args = [q, k, v]
    if lam_init is None:
        out_w = group * hd
    else:
        out_w = hd
        in_specs += [pl.BlockSpec(lam.shape, lambda bi, h, j: (0, 0)),
                     pl.BlockSpec((1, hd), lambda bi, h, j: (0, 0))]
        args += [lam, subln.reshape(1, hd)]
    return pl.pallas_call(
        kern, grid=(b, hkv, s // TM), in_specs=in_specs,
        out_specs=pl.BlockSpec((1, TM, out_w), lambda bi, h, j: (bi, j, h)),
        out_shape=jax.ShapeDtypeStruct((b, s, hkv * out_w), MXU_DTYPE),
        scratch_shapes=[pltpu.VMEM((group * TM, kv_chunk), F32), pltpu.VMEM((group * TM, kv_chunk), F32)],
        compiler_params=_params("arbitrary", "arbitrary", "arbitrary"), name="flash_attention",
    )(*args)


def _attn_out_body(xt, mod, ng, o_ref, wo_ref):
    return _dot(o_ref[0], wo_ref[...])


def _attn_out_layer(x, modt, ng, router, o, w_o):
    b, s, d = x.shape
    specs = [pl.BlockSpec((1, TM, d), lambda bi, j: (bi, j, 0)),
             pl.BlockSpec((d, d), lambda bi, j: (0, 0))]
    return _mixer_call(_attn_out_body, x, modt, ng, router, (o, w_o.astype(MXU_DTYPE)), specs, "attn_out")


def _rope_tables(ctx_len, lat_len, dim, reps):
    n = dim // 4
    pos = jnp.arange(lat_len, dtype=jnp.int32)
    row = (pos // GRID_W).astype(F32)
    col = (pos % GRID_W).astype(F32)
    freqs = ROPE_THETA ** (-jnp.arange(n, dtype=F32) / n)
    ang = jnp.concatenate([row[:, None] * freqs, col[:, None] * freqs], axis=-1)
    ang = jnp.concatenate([jnp.zeros((ctx_len, dim // 2), F32), ang], axis=0)
    cos, sin = jnp.cos(ang), jnp.sin(ang)
    return jnp.tile(cos, (1, 2 * reps)), jnp.tile(jnp.concatenate([-sin, sin], axis=-1), (1, reps))


def _expert_kernel(be_ref, nused_ref, src_ref, src_next_ref, dst_prev_ref, dst_ref, h_hbm,
                   w1_ref, b1_ref, w2_ref, b2_ref, ya_hbm, xbuf0, xbuf1, obuf0, obuf1, gather_sem, scatter_sem):
    del be_ref
    i = pl.program_id(0)
    n_used = nused_ref[0]
    xbufs, obufs = (xbuf0, xbuf1), (obuf0, obuf1)
    rows, f = xbuf0.shape[0], w2_ref.shape[1]

    def gather_row(idx_ref, p, r):
        pltpu.make_async_copy(h_hbm.at[pl.ds(idx_ref[0, 0, r], 1)], xbufs[p].at[pl.ds(r, 1)],
                              gather_sem.at[p]).start()

    def scatter_row(idx_ref, p, r):
        pltpu.make_async_copy(obufs[p].at[pl.ds(r, 1)], ya_hbm.at[pl.ds(idx_ref[0, 0, r], 1)],
                              scatter_sem.at[p]).start()

    def gather_start(idx_ref, p):
        for r in range(rows):
            gather_row(idx_ref, p, r)

    def scatter_start(idx_ref, p):
        for r in range(rows):
            scatter_row(idx_ref, p, r)

    def gather_wait(p):
        pltpu.make_async_copy(xbufs[p], xbufs[p], gather_sem.at[p]).wait()

    def scatter_wait(p):
        pltpu.make_async_copy(obufs[p], obufs[p], scatter_sem.at[p]).wait()

    @pl.when(i == 0)
    def _():
        gather_start(src_ref, 0)
        obuf1[...] = jnp.zeros_like(obuf1)

    def step(p):
        @pl.when(i >= 1)
        def _():
            scatter_wait(p)

        gather_wait(p)
        gather_start(src_next_ref, 1 - p)
        scatter_start(dst_prev_ref, 1 - p)
        gu = _dot(xbufs[p][...], w1_ref[0]) + b1_ref[0]
        g = jnp.minimum(gu[:, :f], SWIGLU_LIMIT)
        u = jnp.clip(gu[:, f:], -SWIGLU_LIMIT, SWIGLU_LIMIT)
        act = g * _sigmoid(SWIGLU_ALPHA * g) * (u + 1.0)
        obufs[p][...] = _dot(act, w2_ref[0]) + b2_ref[0]

        @pl.when(i == n_used - 1)
        def _():
            scatter_start(dst_ref, p)
            gather_wait(1 - p)
            scatter_wait(1 - p)
            scatter_wait(p)

    for p in range(2):
        pl.when((i < n_used) & (i % 2 == p))(functools.partial(step, p))


def _experts(h2, row_src, row_dst, block_e, n_used, w1, b1, w2, b2, n_out_rows):
    t, d = h2.shape
    e, _, f2 = w1.shape
    f = w2.shape[1]
    tb = EXPERT_ROWS
    n_blocks = row_src.shape[0]
    last = n_blocks - 1
    spare = row_dst.shape[0] - 1
    grid_spec = pltpu.PrefetchScalarGridSpec(
        num_scalar_prefetch=2, grid=(n_blocks,),
        in_specs=[pl.BlockSpec((1, 1, tb), lambda i, be, nu: (i, 0, 0), memory_space=pltpu.SMEM),
                  pl.BlockSpec((1, 1, tb), lambda i, be, nu: (jnp.minimum(i + 1, last), 0, 0),
                               memory_space=pltpu.SMEM),
                  pl.BlockSpec((1, 1, tb), lambda i, be, nu: (jnp.where(i == 0, spare, i - 1), 0, 0),
                               memory_space=pltpu.SMEM),
                  pl.BlockSpec((1, 1, tb), lambda i, be, nu: (i, 0, 0), memory_space=pltpu.SMEM),
                  pl.BlockSpec(memory_space=pl.ANY),
                  pl.BlockSpec((1, d, f2), lambda i, be, nu: (be[i], 0, 0)),
                  pl.BlockSpec((1, 1, f2), lambda i, be, nu: (be[i], 0, 0)),
                  pl.BlockSpec((1, f, d), lambda i, be, nu: (be[i], 0, 0)),
                  pl.BlockSpec((1, 1, d), lambda i, be, nu: (be[i], 0, 0))],
        out_specs=pl.BlockSpec(memory_space=pl.ANY),
        scratch_shapes=[pltpu.VMEM((tb, d), F32), pltpu.VMEM((tb, d), F32),
                        pltpu.VMEM((tb, d), F32), pltpu.VMEM((tb, d), F32),
                        pltpu.SemaphoreType.DMA((2,)), pltpu.SemaphoreType.DMA((2,))])
    return pl.pallas_call(
        _expert_kernel, grid_spec=grid_spec,
        out_shape=jax.ShapeDtypeStruct((n_out_rows, d), F32),
        compiler_params=pltpu.CompilerParams(dimension_semantics=("arbitrary",), has_side_effects=True,
                                             vmem_limit_bytes=VMEM_LIMIT_BYTES),
        name="moe_experts",
    )(block_e, n_used, row_src, row_src, row_dst, row_dst, h2, w1, b1.reshape(e, 1, f2), w2, b2.reshape(e, 1, d))


def _combine_kernel(y0_ref, y1_ref, y2_ref, y3_ref, mf_ref, x1_ref, mod_ref, ng_ref, x2_ref):
    gates = mf_ref[...]
    y = gates[:, 0:1] * y0_ref[...]
    for k, yk_ref in enumerate((y1_ref, y2_ref, y3_ref), start=1):
        y = y + gates[:, k:k + 1] * yk_ref[...]
    mod = mod_ref[0]
    x2_ref[0] = x1_ref[0] + mod[5:6] * _rms(y, ng_ref[3:4])


def _combine(ya, mf, x1, modt, ng, latent_only):
    b, s, d = x1.shape
    nt = s // TM
    tiles = b * nt
    if latent_only:
        out_spec = pl.BlockSpec((1, TM, d), lambda bi, j: (bi, jnp.maximum(j - 1, 0), 0))
        out_shape = jax.ShapeDtypeStruct((b, s - TM, d), F32)
    else:
        out_spec = pl.BlockSpec((1, TM, d), lambda bi, j: (bi, j, 0))
        out_shape = jax.ShapeDtypeStruct((b, s, d), F32)
    return pl.pallas_call(
        _combine_kernel, grid=(b, nt),
        in_specs=[pl.BlockSpec((TM, d), functools.partial(lambda k, bi, j: (k * tiles + bi * nt + j, 0), k))
                  for k in range(TOP_K)] + [
                  pl.BlockSpec((TM, LANES), lambda bi, j: (bi * nt + j, 0)),
                  pl.BlockSpec((1, TM, d), lambda bi, j: (bi, j, 0)),
                  pl.BlockSpec((1, SUBLANES, d), lambda bi, j: (bi * 2 + jnp.minimum(j, 1), 0, 0)),
                  pl.BlockSpec((4, d), lambda bi, j: (0, 0))],
        out_specs=out_spec, out_shape=out_shape,
        compiler_params=_params("arbitrary", "arbitrary"), name="moe_combine",
    )(ya, ya, ya, ya, mf, x1, modt, ng)


def _moe(x1, h2, mi, mf, cnt, modt, ng, w1, b1, w2, b2, latent_only):
    t = h2.shape[0]
    tb = EXPERT_ROWS
    counts = cnt[0, :N_EXPERTS].astype(jnp.int32)
    padded = (counts + tb - 1) // tb * tb
    pend = jnp.cumsum(padded)
    pstart = pend - padded
    dest = pstart[mi[:, :TOP_K]] + mi[:, TOP_K:2 * TOP_K]
    n_blocks = -(-(t * TOP_K) // tb) + N_EXPERTS
    block_row0 = jnp.arange(n_blocks, dtype=jnp.int32) * tb
    block_e = jnp.minimum(jnp.sum((pend[None, :] <= block_row0[:, None]).astype(jnp.int32), axis=1),
                          N_EXPERTS - 1)
    n_used = (pend[-1:] // tb).astype(jnp.int32)
    flat_dest = dest.reshape(-1)
    pair = jnp.arange(t * TOP_K, dtype=jnp.int32)
    row_src = jnp.zeros((n_blocks * tb,), jnp.int32).at[flat_dest].set(pair // TOP_K)
    slot_row = jnp.arange((n_blocks + 1) * tb, dtype=jnp.int32)
    spare_rows = t * TOP_K + (slot_row // tb % 2) * tb + slot_row % tb
    row_dst = spare_rows.at[flat_dest].set((pair % TOP_K) * t + pair // TOP_K)
    ya = _experts(h2, row_src.reshape(n_blocks, 1, tb), row_dst.reshape(n_blocks + 1, 1, tb), block_e, n_used,
                  w1, b1, w2, b2, t * TOP_K + 2 * tb)
    return _combine(ya, mf, x1, modt, ng, latent_only)


def kernel(x, c, ctx, c_ctx, mod_w, mod_b, norm_g, pool_w, pool_scale, gqa_w_qkv, gqa_w_o, gqa_q_gain, gqa_k_gain, conv_w_in, conv_w_dw, conv_w_out, diff_w_qkv, diff_w_o, diff_lambda, diff_subln_gain, router_w, router_b, moe_w1, moe_b1, moe_w2, moe_b2):
    b, lat_len, d = x.shape
    ctx_len = ctx.shape[1]
    depth = mod_w.shape[0]
    n_mixers = 4
    assert ctx_len == TM and lat_len % TM == 0 and b + 1 <= SUBLANES and d % LANES == 0

    xa = jnp.concatenate([ctx, x], axis=1)

    cv = jnp.zeros((SUBLANES, d), F32).at[:b].set(c).at[b].set(c_ctx)
    mods = _modulation(cv, mod_w, mod_b).reshape(depth, SUBLANES, MOD_CHUNKS, d)
    modt_all = jnp.stack([jnp.broadcast_to(mods[:, b][:, None], (depth, b, MOD_CHUNKS, d)), mods[:, :b]], axis=2)
    modt_all = jnp.pad(modt_all, ((0, 0), (0, 0), (0, 0), (0, SUBLANES - MOD_CHUNKS), (0, 0)))
    modt_all = modt_all.reshape(depth, b * 2, SUBLANES, d)

    cos_b, sin_b = _rope_tables(ctx_len, lat_len, GQA_HEAD_DIM, 1)
    cos_d, sin_d = _rope_tables(ctx_len, lat_len, DIFF_HEAD_DIM, 2)

    for i in range(depth):
        m, jj = i % n_mixers, i // n_mixers
        modt, ng = modt_all[i], norm_g[i]
        rw = jnp.zeros((d, LANES), F32).at[:, :N_EXPERTS].set(router_w[i])
        rwh = rw.astype(jnp.bfloat16)
        rwl = (rw - rwh.astype(F32)).astype(jnp.bfloat16)
        rb = jnp.full((1, LANES), NEG_BIG, F32).at[0, :N_EXPERTS].set(router_b[i])
        router = (rwh, rwl, rb)
        if m == 0:
            res = _pool_layer(xa, modt, ng, router, pool_w[jj], pool_scale[jj], ctx_len)
        elif m == 1:
            q, k, v = _gqa_qkv(xa, modt, ng, gqa_w_qkv[jj], gqa_q_gain[jj], gqa_k_gain[jj], cos_b, sin_b)
            o = _flash(q, k, v, GQA_Q_HEADS // GQA_KV_HEADS, ctx_len)
            res = _attn_out_layer(xa, modt, ng, router, o, gqa_w_o[jj])
        elif m == 2:
            res = _conv_layer(xa, modt, ng, router, conv_w_in[jj], conv_w_dw[jj], conv_w_out[jj])
        else:
            lam_init = 0.8 - 0.6 * math.exp(-0.3 * i)
            q, k, v = _diff_qkv(xa, modt, ng, diff_w_qkv[jj], cos_d, sin_d)
            o = _flash(q, k, v, 2, ctx_len, lam=diff_lambda[jj], subln=diff_subln_gain[jj], lam_init=lam_init)
            res = _attn_out_layer(xa, modt, ng, router, o, diff_w_o[jj])
        x1, h2, mi, mf, cnt = res
        xa = _moe(x1, h2, mi, mf, cnt, modt, ng,
                  moe_w1[i].astype(MXU_DTYPE), moe_b1[i], moe_w2[i].astype(MXU_DTYPE), moe_b2[i],
                  latent_only=(i == depth - 1))
    return xa
```

```python
import functools
import math

import jax
import jax.numpy as jnp
from jax import lax
from jax.experimental import pallas as pl
from jax.experimental.pallas import tpu as pltpu

RMS_EPS = 1e-6
ROPE_THETA = 10000.0
GRID_W = 64
POOL_WINDOWS = (2, 4, 8, 16)
N_EXPERTS = 32
TOP_K = 4
SWIGLU_LIMIT = 7.0
SWIGLU_ALPHA = 1.702
GQA_HEAD_DIM = 128
GQA_Q_HEADS = 8
GQA_KV_HEADS = 2
DIFF_HEAD_DIM = 64
DIFF_HEADS = 8
MOD_CHUNKS = 6

LANES = 128
SUBLANES = 8
VMEM_LIMIT_BYTES = 56 * 1024 * 1024

TM = 256
EXPERT_ROWS = 512
KV_CHUNK = 2048
SCORE_COLS = 256
LOG2E = math.log2(math.e)
HALO = SUBLANES

MXU_DTYPE = jnp.bfloat16
F32 = jnp.float32
NEG_BIG = -1e30


def _params(*sem):
    return pltpu.CompilerParams(dimension_semantics=sem, vmem_limit_bytes=VMEM_LIMIT_BYTES)


def _dot(a, b):
    return jnp.dot(a.astype(MXU_DTYPE), b.astype(MXU_DTYPE), preferred_element_type=F32)


def _rms(x, g):
    return x * lax.rsqrt(jnp.mean(x * x, axis=-1, keepdims=True) + RMS_EPS) * g


def _sigmoid(z):
    return 1.0 / (1.0 + jnp.exp(-z))


def _shift_up(a, s):
    n = a.shape[0]
    return pltpu.roll(a, (n - s) % n, 0)


def _mod_kernel(cv_ref, w_ref, b_ref, o_ref):
    a = cv_ref[...]
    o_ref[0] = _dot(a * _sigmoid(a), w_ref[0]) + b_ref[0]


def _modulation(cv, mod_w, mod_b):
    depth, d, n = mod_w.shape
    tn = n // 4
    return pl.pallas_call(
        _mod_kernel,
        grid=(depth, n // tn),
        in_specs=[pl.BlockSpec((SUBLANES, d), lambda i, k: (0, 0)),
                  pl.BlockSpec((1, d, tn), lambda i, k: (i, 0, k)),
                  pl.BlockSpec((1, 1, tn), lambda i, k: (i, 0, k))],
        out_specs=pl.BlockSpec((1, SUBLANES, tn), lambda i, k: (i, 0, k)),
        out_shape=jax.ShapeDtypeStruct((depth, SUBLANES, n), F32),
        compiler_params=_params("arbitrary", "arbitrary"),
        name="modulation",
    )(cv, mod_w, mod_b.reshape(depth, 1, n))


def _post_mixer(y, x, mod, ng, rwh_ref, rwl_ref, rb_ref, base_ref,
                x1_ref, h2_ref, mi_ref, mf_ref, cnt_ref):
    tm = x.shape[0]
    x1 = x + mod[2:3] * _rms(y, ng[1:2])
    h2 = _rms(x1, ng[2:3]) * (1.0 + mod[4:5]) + mod[3:4]
    x1_ref[0] = x1
    h2_ref[...] = h2

    h_hi = h2.astype(jnp.bfloat16)
    h_lo = (h2 - h_hi.astype(F32)).astype(jnp.bfloat16)
    rwh = rwh_ref[...]
    logits = (jnp.dot(h_hi, rwh, preferred_element_type=F32)
              + jnp.dot(h_lo, rwh, preferred_element_type=F32)
              + jnp.dot(h_hi, rwl_ref[...], preferred_element_type=F32)) + rb_ref[...]

    lane = lax.broadcasted_iota(jnp.int32, (tm, LANES), 1)
    lane_f = lane.astype(F32)
    vals, idxs, hots = [], [], []
    l = logits
    for _ in range(TOP_K):
        m = jnp.max(l, axis=-1, keepdims=True)
        idx = jnp.min(jnp.where(l == m, lane_f, float(LANES)), axis=-1, keepdims=True)
        hot = lane_f == idx
        vals.append(m)
        idxs.append(idx)
        hots.append(hot)
        l = jnp.where(hot, -3e38, l)
    exps = [jnp.exp(v - vals[0]) for v in vals]
    den = exps[0] + exps[1] + exps[2] + exps[3]

    @pl.when((pl.program_id(0) == 0) & (pl.program_id(1) == 0))
    def _():
        base_ref[...] = jnp.zeros_like(base_ref)

    hot_all = (hots[0].astype(F32) + hots[1].astype(F32) + hots[2].astype(F32) + hots[3].astype(F32))
    rows = lax.broadcasted_iota(jnp.int32, (tm, tm), 0)
    cols = lax.broadcasted_iota(jnp.int32, (tm, tm), 1)
    lower = (rows > cols).astype(jnp.bfloat16)
    before = jnp.dot(lower, hot_all.astype(jnp.bfloat16), preferred_element_type=F32) + base_ref[...]
    mi = jnp.zeros((tm, LANES), F32)
    mf = jnp.zeros((tm, LANES), F32)
    for k in range(TOP_K):
        rank = jnp.sum(jnp.where(hots[k], before, 0.0), axis=-1, keepdims=True)
        mi = mi + jnp.where(lane == k, idxs[k], 0.0) + jnp.where(lane == TOP_K + k, rank, 0.0)
        mf = mf + jnp.where(lane == k, exps[k] / den, 0.0)
    mi_ref[...] = mi.astype(jnp.int32)
    mf_ref[...] = mf
    base_ref[...] = base_ref[...] + jnp.sum(hot_all, axis=0, keepdims=True)
    cnt_ref[...] = base_ref[...]


def _mixer_call(body, x, modt, ng, router, extra_args, extra_specs, name):
    b, s, d = x.shape
    nt = s // TM
    rwh, rwl, rb = router
    n_extra = len(extra_args)

    def kern(*refs):
        x_ref, mod_ref, ng_ref, rwh_ref, rwl_ref, rb_ref = refs[:6]
        extra = refs[6:6 + n_extra]
        x1_ref, h2_ref, mi_ref, mf_ref, cnt_ref, base_ref = refs[6 + n_extra:]
        xt = x_ref[0]
        mod = mod_ref[0]
        ngv = ng_ref[...]
        y = body(xt, mod, ngv, *extra)
        _post_mixer(y, xt, mod, ngv, rwh_ref, rwl_ref, rb_ref, base_ref,
                    x1_ref, h2_ref, mi_ref, mf_ref, cnt_ref)

    in_specs = [
        pl.BlockSpec((1, TM, d), lambda bi, j: (bi, j, 0)),
        pl.BlockSpec((1, SUBLANES, d), lambda bi, j: (bi * 2 + jnp.minimum(j, 1), 0, 0)),
        pl.BlockSpec((4, d), lambda bi, j: (0, 0)),
        pl.BlockSpec((d, LANES), lambda bi, j: (0, 0)),
        pl.BlockSpec((d, LANES), lambda bi, j: (0, 0)),
        pl.BlockSpec((1, LANES), lambda bi, j: (0, 0)),
    ] + list(extra_specs)
    out_specs = [
        pl.BlockSpec((1, TM, d), lambda bi, j: (bi, j, 0)),
        pl.BlockSpec((TM, d), lambda bi, j: (bi * nt + j, 0)),
        pl.BlockSpec((TM, LANES), lambda bi, j: (bi * nt + j, 0)),
        pl.BlockSpec((TM, LANES), lambda bi, j: (bi * nt + j, 0)),
        pl.BlockSpec((1, LANES), lambda bi, j: (0, 0)),
    ]
    out_shape = [
        jax.ShapeDtypeStruct((b, s, d), F32),
        jax.ShapeDtypeStruct((b * s, d), F32),
        jax.ShapeDtypeStruct((b * s, LANES), jnp.int32),
        jax.ShapeDtypeStruct((b * s, LANES), F32),
        jax.ShapeDtypeStruct((1, LANES), F32),
    ]
    return pl.pallas_call(
        kern, grid=(b, nt), in_specs=in_specs, out_specs=out_specs, out_shape=out_shape,
        scratch_shapes=[pltpu.VMEM((1, LANES), F32)],
        compiler_params=_params("arbitrary", "arbitrary"), name=name,
    )(x, modt, ng, rwh, rwl, rb, *extra_args)


def _halo_specs(s, d):
    last = s // HALO - 1
    per_tile = TM // HALO
    prev = pl.BlockSpec((1, HALO, d), lambda bi, j: (bi, jnp.maximum(j * per_tile - 1, 0), 0))
    nxt = pl.BlockSpec((1, HALO, d), lambda bi, j: (bi, jnp.minimum((j + 1) * per_tile, last), 0))
    return prev, nxt


def _halo_valid():
    j = pl.program_id(1)
    nt = pl.num_programs(1)
    return j >= 2, (j >= 1) & (j <= nt - 2)


def _pool_body(ctx_len, lat_len, xt, mod, ng, xp_ref, xn_ref, pw_ref, ps_ref):
    tm, d = xt.shape
    group = d // len(POOL_WINDOWS)
    prev_ok, next_ok = _halo_valid()
    pre = lambda rows: _rms(rows, ng[0:1]) * (1.0 + mod[1:2]) + mod[0:1]
    hc = pre(xt)
    hp = jnp.where(prev_ok, pre(xp_ref[0]), 0.0)
    hn = jnp.where(next_ok, pre(xn_ref[0]), 0.0)
    ext = jnp.concatenate([hp, hc, hn], axis=0)

    j = pl.program_id(1)
    pos = lax.broadcasted_iota(jnp.int32, (tm, 1), 0) + jnp.where(j == 0, 0, (j - 1) * tm)
    seq_len = jnp.where(j == 0, ctx_len, lat_len)
    outs = []
    for g, w in enumerate(POOL_WINDOWS):
        e = ext[:, g * group:(g + 1) * group]
        run, span = e, 1
        while span < w:
            run = run + _shift_up(run, span)
            span *= 2
        win = _shift_up(run, HALO - w // 2)[:tm] if w // 2 != HALO else run[:tm]
        lo = jnp.maximum(pos - w // 2, 0)
        hi = jnp.minimum(pos + (w - w // 2), seq_len)
        diff = win / (hi - lo).astype(F32) - hc[:, g * group:(g + 1) * group]
        outs.append(_dot(diff, pw_ref[g]))
    return jnp.concatenate(outs, axis=1) * ps_ref[...]


def _pool_layer(x, modt, ng, router, pool_w, pool_scale, ctx_len):
    b, s, d = x.shape
    prev, nxt = _halo_specs(s, d)
    g = len(POOL_WINDOWS)
    specs = [prev, nxt,
             pl.BlockSpec((g, d // g, d // g), lambda bi, j: (0, 0, 0)),
             pl.BlockSpec((1, d), lambda bi, j: (0, 0))]
    body = functools.partial(_pool_body, ctx_len, s - ctx_len)
    return _mixer_call(body, x, modt, ng, router,
                       (x, x, pool_w.astype(MXU_DTYPE), pool_scale.reshape(1, d)), specs, "pool_mixer")


def _conv_body(xt, mod, ng, xp_ref, xn_ref, win_ref, wdw_ref, wout_ref):
    tm, d = xt.shape
    prev_ok, next_ok = _halo_valid()
    rows = jnp.concatenate([xp_ref[0], xt, xn_ref[0]], axis=0)
    h = _rms(rows, ng[0:1]) * (1.0 + mod[1:2]) + mod[0:1]
    proj = _dot(h, win_ref[...])
    gate_b = proj[HALO:HALO + tm, :d]
    u = proj[:, d:2 * d] * proj[:, 2 * d:]
    r = lax.broadcasted_iota(jnp.int32, (tm + 2 * HALO, 1), 0)
    keep = ((r >= HALO) | prev_ok) & ((r < HALO + tm) | next_ok)
    u = jnp.where(keep, u, 0.0)
    wdw = wdw_ref[...]
    conv = (wdw[0:1] * _shift_up(u, HALO - 1)[:tm] + wdw[1:2] * u[HALO:HALO + tm]
            + wdw[2:3] * _shift_up(u, HALO + 1)[:tm])
    return _dot(gate_b * conv, wout_ref[...])


def _conv_layer(x, modt, ng, router, w_in, w_dw, w_out):
    b, s, d = x.shape
    prev, nxt = _halo_specs(s, d)
    specs = [prev, nxt,
             pl.BlockSpec((d, 3 * d), lambda bi, j: (0, 0)),
             pl.BlockSpec((SUBLANES, d), lambda bi, j: (0, 0)),
             pl.BlockSpec((d, d), lambda bi, j: (0, 0))]
    w_dw8 = jnp.zeros((SUBLANES, d), F32).at[:w_dw.shape[0]].set(w_dw)
    return _mixer_call(_conv_body, x, modt, ng, router,
                       (x, x, w_in.astype(MXU_DTYPE), w_dw8, w_out.astype(MXU_DTYPE)), specs, "conv_mixer")


def _gqa_qkv_kernel(x_ref, mod_ref, ng_ref, w_ref, qg_ref, kg_ref, cos_ref, sin_ref,
                    q_ref, k_ref, v_ref):
    mod = mod_ref[0]
    h = _rms(x_ref[0], ng_ref[0:1]) * (1.0 + mod[1:2]) + mod[0:1]
    qkv = _dot(h, w_ref[...])
    cos, sin = cos_ref[...], sin_ref[...]
    hd = GQA_HEAD_DIM

    def rope(t):
        return t * cos + pltpu.roll(t, hd // 2, 1) * sin

    scale = hd ** -0.5 * LOG2E
    for i in range(GQA_Q_HEADS):
        q = rope(_rms(qkv[:, i * hd:(i + 1) * hd], qg_ref[...]))
        q_ref[0, i] = (q * scale).astype(q_ref.dtype)
    for i in range(GQA_KV_HEADS):
        o = (GQA_Q_HEADS + i) * hd
        k_ref[0, i] = rope(_rms(qkv[:, o:o + hd], kg_ref[...])).T.astype(k_ref.dtype)
        o = (GQA_Q_HEADS + GQA_KV_HEADS + i) * hd
        v_ref[0, i] = qkv[:, o:o + hd].astype(v_ref.dtype)


def _gqa_qkv(x, modt, ng, w_qkv, q_gain, k_gain, cos, sin):
    b, s, d = x.shape
    hd = GQA_HEAD_DIM
    n = w_qkv.shape[1]
    return pl.pallas_call(
        _gqa_qkv_kernel, grid=(b, s // TM),
        in_specs=[pl.BlockSpec((1, TM, d), lambda bi, j: (bi, j, 0)),
                  pl.BlockSpec((1, SUBLANES, d), lambda bi, j: (bi * 2 + jnp.minimum(j, 1), 0, 0)),
                  pl.BlockSpec((4, d), lambda bi, j: (0, 0)),
                  pl.BlockSpec((d, n), lambda bi, j: (0, 0)),
                  pl.BlockSpec((1, hd), lambda bi, j: (0, 0)),
                  pl.BlockSpec((1, hd), lambda bi, j: (0, 0)),
                  pl.BlockSpec((TM, hd), lambda bi, j: (j, 0)),
                  pl.BlockSpec((TM, hd), lambda bi, j: (j, 0))],
        out_specs=[pl.BlockSpec((1, GQA_Q_HEADS, TM, hd), lambda bi, j: (bi, 0, j, 0)),
                   pl.BlockSpec((1, GQA_KV_HEADS, hd, TM), lambda bi, j: (bi, 0, 0, j)),
                   pl.BlockSpec((1, GQA_KV_HEADS, TM, hd), lambda bi, j: (bi, 0, j, 0))],
        out_shape=[jax.ShapeDtypeStruct((b, GQA_Q_HEADS, s, hd), MXU_DTYPE),
                   jax.ShapeDtypeStruct((b, GQA_KV_HEADS, hd, s), MXU_DTYPE),
                   jax.ShapeDtypeStruct((b, GQA_KV_HEADS, s, hd), MXU_DTYPE)],
        compiler_params=_params("arbitrary", "arbitrary"), name="gqa_qkv",
    )(x, modt, ng, w_qkv.astype(MXU_DTYPE), q_gain.reshape(1, hd), k_gain.reshape(1, hd), cos, sin)


def _diff_qkv_kernel(x_ref, mod_ref, ng_ref, w_ref, cos_ref, sin_ref, q_ref, k_ref, v_ref):
    mod = mod_ref[0]
    d = x_ref.shape[2]
    h = _rms(x_ref[0], ng_ref[0:1]) * (1.0 + mod[1:2]) + mod[0:1]
    qkv = _dot(h, w_ref[...])
    cos, sin = cos_ref[...], sin_ref[...]
    hw = 2 * DIFF_HEAD_DIM
    lane = lax.broadcasted_iota(jnp.int32, (x_ref.shape[1], hw), 1)
    quarter = DIFF_HEAD_DIM // 2
    take_up = (lane // quarter) % 2 == 0

    def rope(t):
        rot = jnp.where(take_up, pltpu.roll(t, hw - quarter, 1), pltpu.roll(t, quarter, 1))
        return t * cos + rot * sin

    scale = DIFF_HEAD_DIM ** -0.5 * LOG2E
    first = lane < DIFF_HEAD_DIM
    for i in range(DIFF_HEADS):
        q = rope(qkv[:, i * hw:(i + 1) * hw]) * scale
        q_ref[0, 2 * i] = jnp.where(first, q, 0.0).astype(q_ref.dtype)
        q_ref[0, 2 * i + 1] = jnp.where(first, 0.0, q).astype(q_ref.dtype)
        k_ref[0, i] = rope(qkv[:, d + i * hw:d + (i + 1) * hw]).T.astype(k_ref.dtype)
        v_ref[0, i] = qkv[:, 2 * d + i * hw:2 * d + (i + 1) * hw].astype(v_ref.dtype)


def _diff_qkv(x, modt, ng, w_qkv, cos, sin):
    b, s, d = x.shape
    hw = 2 * DIFF_HEAD_DIM
    n = w_qkv.shape[1]
    return pl.pallas_call(
        _diff_qkv_kernel, grid=(b, s // TM),
        in_specs=[pl.BlockSpec((1, TM, d), lambda bi, j: (bi, j, 0)),
                  pl.BlockSpec((1, SUBLANES, d), lambda bi, j: (bi * 2 + jnp.minimum(j, 1), 0, 0)),
                  pl.BlockSpec((4, d), lambda bi, j: (0, 0)),
                  pl.BlockSpec((d, n), lambda bi, j: (0, 0)),
                  pl.BlockSpec((TM, hw), lambda bi, j: (j, 0)),
                  pl.BlockSpec((TM, hw), lambda bi, j: (j, 0))],
        out_specs=[pl.BlockSpec((1, 2 * DIFF_HEADS, TM, hw), lambda bi, j: (bi, 0, j, 0)),
                   pl.BlockSpec((1, DIFF_HEADS, hw, TM), lambda bi, j: (bi, 0, 0, j)),
                   pl.BlockSpec((1, DIFF_HEADS, TM, hw), lambda bi, j: (bi, 0, j, 0))],
        out_shape=[jax.ShapeDtypeStruct((b, 2 * DIFF_HEADS, s, hw), MXU_DTYPE),
                   jax.ShapeDtypeStruct((b, DIFF_HEADS, hw, s), MXU_DTYPE),
                   jax.ShapeDtypeStruct((b, DIFF_HEADS, s, hw), MXU_DTYPE)],
        compiler_params=_params("arbitrary", "arbitrary"), name="diff_qkv",
    )(x, modt, ng, w_qkv.astype(MXU_DTYPE), cos, sin)


def _flash_kernel(*refs, group, ctx_len, kv_chunk, n_chunks, lam_init):
    if lam_init is None:
        q_ref, kt_ref, v_ref, o_ref, s_even, s_odd = refs
    else:
        q_ref, kt_ref, v_ref, lam_ref, sg_ref, o_ref, s_even, s_odd = refs
    s_bufs = (s_even, s_odd)
    tq, hd = q_ref.shape[2], q_ref.shape[3]
    j = pl.program_id(2)
    rows = group * tq
    q = q_ref[0].reshape(rows, hd)
    n_sub = kv_chunk // SCORE_COLS

    def qk(start):
        return jnp.dot(q, kt_ref[0, 0, :, pl.ds(start, SCORE_COLS)], preferred_element_type=F32)

    def probs(s, m):
        return jnp.exp2((s - m).astype(v_ref.dtype))

    def pv(p, start):
        v = v_ref[0, 0, pl.ds(start, SCORE_COLS), :]
        return jnp.dot(p, jnp.concatenate([v, jnp.ones_like(v)], axis=1), preferred_element_type=F32)

    def fold(x):
        return [x[:, c * LANES:(c + 1) * LANES] for c in range(SCORE_COLS // LANES)]

    def chunk_start(i):
        start = ctx_len + i * kv_chunk
        return start if isinstance(i, int) else pl.multiple_of(start, SCORE_COLS)

    s = qk(0)
    m = jnp.max(s, axis=-1, keepdims=True)
    acc = pv(probs(s, m), 0)

    def latent_keys(state):
        def scores_and_max(i, buf, c, mx):
            sc = qk(chunk_start(i) + c * SCORE_COLS)
            buf[:, c * SCORE_COLS:(c + 1) * SCORE_COLS] = sc
            for part in fold(sc):
                mx = part if mx is None else jnp.maximum(mx, part)
            return mx

        def new_max(m, mx):
            m_new = jnp.maximum(m, jnp.max(mx, axis=-1, keepdims=True))
            return m_new, jnp.exp2(m - m_new)

        def step(i, parity, has_next, m, alpha, acc):
            acc = alpha * acc
            mx = None
            for c in range(n_sub):
                if has_next:
                    mx = scores_and_max(i + 1, s_bufs[1 - parity], c, mx)
                pc = probs(s_bufs[parity][:, c * SCORE_COLS:(c + 1) * SCORE_COLS], m)
                acc = acc + pv(pc, chunk_start(i) + c * SCORE_COLS)
            if has_next:
                m, alpha = new_max(m, mx)
            return m, alpha, acc

        m, acc = state
        mx = None
        for c in range(n_sub):
            mx = scores_and_max(0, s_bufs[0], c, mx)
        m, alpha = new_max(m, mx)
        pairs = (n_chunks - 1) // 2

        def body(t, carry):
            carry = step(2 * t, 0, True, *carry)
            return step(2 * t + 1, 1, True, *carry)

        carry = lax.fori_loop(0, pairs, body, (m, alpha, acc))
        for i in range(2 * pairs, n_chunks):
            carry = step(i, i % 2, i < n_chunks - 1, *carry)
        return carry[0], carry[2]

    _, acc = lax.cond(j > 0, latent_keys, lambda state: state, (m, acc))
    o = acc[:, :hd] / acc[:, hd:]
    if lam_init is None:
        for g in range(group):
            o_ref[0, :, g * hd:(g + 1) * hd] = o[g * tq:(g + 1) * tq].astype(o_ref.dtype)
    else:
        lv = lam_ref[...]
        lam = (jnp.exp(jnp.sum(lv[0:1] * lv[1:2], axis=-1, keepdims=True))
               - jnp.exp(jnp.sum(lv[2:3] * lv[3:4], axis=-1, keepdims=True)) + lam_init)
        od = o[:tq] - lam * o[tq:]
        o_ref[0] = (_rms(od, sg_ref[...]) * (1.0 - lam_init)).astype(o_ref.dtype)


def _flash(q, k, v, group, ctx_len, lam=None, subln=None, lam_init=None):
    b, hq, s, hd = q.shape
    hkv = k.shape[1]
    lat = s - ctx_len
    kv_chunk = min(KV_CHUNK, lat)
    assert ctx_len == SCORE_COLS and kv_chunk % SCORE_COLS == 0 and lat % kv_chunk == 0
    kern =functools.partial(_flash_kernel, group=group, ctx_len=ctx_len, kv_chunk=kv_chunk,
                             n_chunks=lat // kv_chunk, lam_init=lam_init)
    in_specs = [pl.BlockSpec((1, group, TM, hd), lambda bi, h, j: (bi, h, j, 0)),
                pl.BlockSpec((1, 1, hd, s), lambda bi, h, j: (bi, h, 0, 0)),
                pl.BlockSpec((1, 1, s, hd), lambda bi, h, j: (bi, h, 0, 0))]
    args = [q, k, v]
    if lam_init is None:
        out_w = group * hd
    else:
        out_w = hd
        in_specs += [pl.BlockSpec(lam.shape, lambda bi, h, j: (0, 0)),
                     pl.BlockSpec((1, hd), lambda bi, h, j: (0, 0))]
        args += [lam, subln.reshape(1, hd)]
    return pl.pallas_call(
        kern, grid=(b, hkv, s // TM), in_specs=in_specs,
        out_specs=pl.BlockSpec((1, TM, out_w), lambda bi, h, j: (bi, j, h)),
        out_shape=jax.ShapeDtypeStruct((b, s, hkv * out_w), MXU_DTYPE),
        scratch_shapes=[pltpu.VMEM((group * TM, kv_chunk), F32), pltpu.VMEM((group * TM, kv_chunk), F32)],
        compiler_params=_params("arbitrary", "arbitrary", "arbitrary"), name="flash_attention",
    )(*args)


def _attn_out_body(xt, mod, ng, o_ref, wo_ref):
    return _dot(o_ref[0], wo_ref[...])


def _attn_out_layer(x, modt, ng, router, o, w_o):
    b, s, d = x.shape
    specs = [pl.BlockSpec((1, TM, d), lambda bi, j: (bi, j, 0)),
             pl.BlockSpec((d, d), lambda bi, j: (0, 0))]
    return _mixer_call(_attn_out_body, x, modt, ng, router, (o, w_o.astype(MXU_DTYPE)), specs, "attn_out")


def _rope_tables(ctx_len, lat_len, dim, reps):
    n = dim // 4
    pos = jnp.arange(lat_len, dtype=jnp.int32)
    row = (pos // GRID_W).astype(F32)
    col = (pos % GRID_W).astype(F32)
    freqs = ROPE_THETA ** (-jnp.arange(n, dtype=F32) / n)
    ang = jnp.concatenate([row[:, None] * freqs, col[:, None] * freqs], axis=-1)
    ang = jnp.concatenate([jnp.zeros((ctx_len, dim // 2), F32), ang], axis=0)
    cos, sin = jnp.cos(ang), jnp.sin(ang)
    return jnp.tile(cos, (1, 2 * reps)), jnp.tile(jnp.concatenate([-sin, sin], axis=-1), (1, reps))


def _dispatch_kernel(tail_ref, dest_ref, h_ref, hs_hbm, zero_buf, sem):
    @pl.when(pl.program_id(0) == 0)
    def _():
        zero_buf[...] = jnp.zeros_like(zero_buf)

        def tail_copy(e):
            start = pl.multiple_of(tail_ref[e], SUBLANES)
            return pltpu.make_async_copy(zero_buf, hs_hbm.at[pl.ds(start, EXPERT_ROWS)], sem)

        for e in range(N_EXPERTS):
            tail_copy(e).start()
        for e in range(N_EXPERTS):
            tail_copy(e).wait()

    for r in range(TM):
        for k in range(TOP_K):
            pltpu.make_async_copy(h_ref.at[pl.ds(r, 1)], hs_hbm.at[pl.ds(dest_ref[0, 0, r * TOP_K + k], 1)],
                                  sem).start()
    for _ in range(TOP_K):
        pltpu.make_async_copy(h_ref, h_ref, sem).wait()


def _dispatch(h2, dest3, tail_start, n_pad):
    t, d = h2.shape
    grid_spec = pltpu.PrefetchScalarGridSpec(
        num_scalar_prefetch=1, grid=(t // TM,),
        in_specs=[pl.BlockSpec((1, 1, TM * TOP_K), lambda i, tail: (i, 0, 0), memory_space=pltpu.SMEM),
                  pl.BlockSpec((TM, d), lambda i, tail: (i, 0))],
        out_specs=pl.BlockSpec(memory_space=pl.ANY),
        scratch_shapes=[pltpu.VMEM((EXPERT_ROWS, d), F32), pltpu.SemaphoreType.DMA])
    return pl.pallas_call(
        _dispatch_kernel, grid_spec=grid_spec,
        out_shape=jax.ShapeDtypeStruct((n_pad, d), F32),
        compiler_params=pltpu.CompilerParams(dimension_semantics=("arbitrary",), has_side_effects=True),
        name="moe_dispatch",
    )(tail_start, dest3, h2)


def _expert_kernel(be_ref, nused_ref, hs_ref, w1_ref, b1_ref, w2_ref, b2_ref, o_ref, w1_mxu, w2_mxu):
    i = pl.program_id(0)
    f = w2_ref.shape[1]

    @pl.when((i == 0) | (be_ref[i] != be_ref[jnp.maximum(i - 1, 0)]))
    def _():
        w1_mxu[...] = w1_ref[0].astype(w1_mxu.dtype)
        w2_mxu[...] = w2_ref[0].astype(w2_mxu.dtype)

    @pl.when(i < nused_ref[0])
    def _():
        gu = _dot(hs_ref[...], w1_mxu[...]) + b1_ref[0]
        g = jnp.minimum(gu[:, :f], SWIGLU_LIMIT)
        u = jnp.clip(gu[:, f:], -SWIGLU_LIMIT, SWIGLU_LIMIT)
        act = g * _sigmoid(SWIGLU_ALPHA * g) * (u + 1.0)
        o_ref[...] = _dot(act, w2_mxu[...]) + b2_ref[0]

    @pl.when(i >= nused_ref[0])
    def _():
        o_ref[...] = jnp.zeros_like(o_ref)


def _experts(hs, block_e, n_used, w1, b1, w2, b2):
    n_pad, d = hs.shape
    e, _, f2 = w1.shape
    f = w2.shape[1]
    tb = EXPERT_ROWS
    grid_spec = pltpu.PrefetchScalarGridSpec(
        num_scalar_prefetch=2, grid=(n_pad // tb,),
        in_specs=[pl.BlockSpec((tb, d), lambda i, be, nu: (i, 0)),
                  pl.BlockSpec((1, d, f2), lambda i, be, nu: (be[i], 0, 0)),
                  pl.BlockSpec((1, 1, f2), lambda i, be, nu: (be[i], 0, 0)),
                  pl.BlockSpec((1, f, d), lambda i, be, nu: (be[i], 0, 0)),
                  pl.BlockSpec((1, 1, d), lambda i, be, nu: (be[i], 0, 0))],
        out_specs=pl.BlockSpec((tb, d), lambda i, be, nu: (i, 0)),
        scratch_shapes=[pltpu.VMEM((d, f2), MXU_DTYPE), pltpu.VMEM((f, d), MXU_DTYPE)])
    return pl.pallas_call(
        _expert_kernel, grid_spec=grid_spec,
        out_shape=jax.ShapeDtypeStruct((n_pad, d), F32),
        compiler_params=_params("arbitrary"), name="moe_experts",
    )(block_e, n_used, hs, w1, b1.reshape(e, 1, f2), w2, b2.reshape(e, 1, d))


def _combine_kernel(dest_ref, ys_hbm, mf_ref, x1_ref, mod_ref, ng_ref, x2_ref, buf, sem):
    for r in range(TM):
        for k in range(TOP_K):
            pltpu.make_async_copy(ys_hbm.at[pl.ds(dest_ref[0, 0, r * TOP_K + k], 1)], buf.at[k, pl.ds(r, 1)],
                                  sem).start()
    pltpu.make_async_copy(buf, buf, sem).wait()
    gates = mf_ref[...]
    y = gates[:, 0:1] * buf[0]
    for k in range(1, TOP_K):
        y = y + gates[:, k:k + 1] * buf[k]
    mod = mod_ref[0]
    x2_ref[0] = x1_ref[0] + mod[5:6] * _rms(y, ng_ref[3:4])


def _combine(ys, dest3, mf, x1, modt, ng, latent_only):
    b, s, d = x1.shape
    nt = s // TM
    if latent_only:
        out_spec = pl.BlockSpec((1, TM, d), lambda bi, j: (bi, jnp.maximum(j - 1, 0), 0))
        out_shape = jax.ShapeDtypeStruct((b, s - TM, d), F32)
    else:
        out_spec = pl.BlockSpec((1, TM, d), lambda bi, j: (bi, j, 0))
        out_shape = jax.ShapeDtypeStruct((b, s, d), F32)
    return pl.pallas_call(
        _combine_kernel, grid=(b, nt),
        in_specs=[pl.BlockSpec((1, 1, TM * TOP_K), lambda bi, j: (bi * nt + j, 0, 0), memory_space=pltpu.SMEM),
                  pl.BlockSpec(memory_space=pl.ANY),
                  pl.BlockSpec((TM, LANES), lambda bi, j: (bi * nt + j, 0)),
                  pl.BlockSpec((1, TM, d), lambda bi, j: (bi, j, 0)),
                  pl.BlockSpec((1, SUBLANES, d), lambda bi, j: (bi * 2 + jnp.minimum(j, 1), 0, 0)),
                  pl.BlockSpec((4, d), lambda bi, j: (0, 0))],
        out_specs=out_spec, out_shape=out_shape,
        scratch_shapes=[pltpu.VMEM((TOP_K, TM, d), F32), pltpu.SemaphoreType.DMA],
        compiler_params=_params("arbitrary", "arbitrary"), name="moe_combine",
    )(dest3, ys, mf, x1, modt, ng)


def _moe(x1, h2, mi, mf, cnt, modt, ng, w1, b1, w2, b2, latent_only):
    t = h2.shape[0]
    tb = EXPERT_ROWS
    counts = cnt[0, :N_EXPERTS].astype(jnp.int32)
    padded = (counts + tb - 1) // tb * tb
    pend = jnp.cumsum(padded)
    pstart = pend - padded
    dest = pstart[mi[:, :TOP_K]] + mi[:, TOP_K:2 * TOP_K]
    n_blocks = -(-(t * TOP_K) // tb) + N_EXPERTS
    block_row0 = jnp.arange(n_blocks, dtype=jnp.int32) * tb
    block_e = jnp.minimum(jnp.sum((pend[None, :] <= block_row0[:, None]).astype(jnp.int32), axis=1),
                          N_EXPERTS - 1)
    n_used = (pend[-1:] // tb).astype(jnp.int32)
    dest3 = dest.reshape(t // TM, 1, TM * TOP_K)
    tail_start = jnp.maximum(pend - tb, 0).astype(jnp.int32)
    hs = _dispatch(h2, dest3, tail_start, n_blocks * tb)
    ys = _experts(hs, block_e, n_used, w1, b1, w2, b2)
    return _combine(ys, dest3, mf, x1, modt, ng, latent_only)


def kernel(x, c, ctx, c_ctx, mod_w, mod_b, norm_g, pool_w, pool_scale, gqa_w_qkv, gqa_w_o, gqa_q_gain, gqa_k_gain, conv_w_in, conv_w_dw, conv_w_out, diff_w_qkv, diff_w_o, diff_lambda, diff_subln_gain, router_w, router_b, moe_w1, moe_b1, moe_w2, moe_b2):
    b, lat_len, d = x.shape
    ctx_len = ctx.shape[1]
    depth = mod_w.shape[0]
    n_mixers = 4
    assert ctx_len == TM and lat_len % TM == 0 and b + 1 <= SUBLANES and d % LANES == 0

    xa = jnp.concatenate([ctx, x], axis=1)

    cv = jnp.zeros((SUBLANES, d), F32).at[:b].set(c).at[b].set(c_ctx)
    mods = _modulation(cv, mod_w, mod_b).reshape(depth, SUBLANES, MOD_CHUNKS, d)
    modt_all = jnp.stack([jnp.broadcast_to(mods[:, b][:, None], (depth, b, MOD_CHUNKS, d)), mods[:, :b]], axis=2)
    modt_all = jnp.pad(modt_all, ((0, 0), (0, 0), (0, 0), (0, SUBLANES - MOD_CHUNKS), (0, 0)))
    modt_all = modt_all.reshape(depth, b * 2, SUBLANES, d)

    cos_b, sin_b = _rope_tables(ctx_len, lat_len, GQA_HEAD_DIM, 1)
    cos_d, sin_d = _rope_tables(ctx_len, lat_len, DIFF_HEAD_DIM, 2)

    for i in range(depth):
        m, jj = i % n_mixers, i // n_mixers
        modt, ng = modt_all[i], norm_g[i]
        rw = jnp.zeros((d, LANES), F32).at[:, :N_EXPERTS].set(router_w[i])
        rwh = rw.astype(jnp.bfloat16)
        rwl = (rw - rwh.astype(F32)).astype(jnp.bfloat16)
        rb = jnp.full((1, LANES), NEG_BIG, F32).at[0, :N_EXPERTS].set(router_b[i])
        router = (rwh, rwl, rb)
        if m == 0:
            res = _pool_layer(xa, modt, ng, router, pool_w[jj], pool_scale[jj], ctx_len)
        elif m == 1:
            q, k, v = _gqa_qkv(xa, modt, ng, gqa_w_qkv[jj], gqa_q_gain[jj], gqa_k_gain[jj], cos_b, sin_b)
            o = _flash(q, k, v, GQA_Q_HEADS // GQA_KV_HEADS, ctx_len)
            res = _attn_out_layer(xa, modt, ng, router, o, gqa_w_o[jj])
        elif m == 2:
            res = _conv_layer(xa, modt, ng, router, conv_w_in[jj], conv_w_dw[jj], conv_w_out[jj])
        else:
            lam_init = 0.8 - 0.6 * math.exp(-0.3 * i)
            q, k, v = _diff_qkv(xa, modt, ng, diff_w_qkv[jj], cos_d, sin_d)
            o = _flash(q, k, v, 2, ctx_len, lam=diff_lambda[jj], subln=diff_subln_gain[jj], lam_init=lam_init)
            res = _attn_out_layer(xa, modt, ng, router, o, diff_w_o[jj])
        x1, h2, mi, mf, cnt = res
        xa = _moe(x1, h2, mi, mf, cnt, modt, ng,
                  moe_w1[i], moe_b1[i], moe_w2[i], moe_b2[i],
                  latent_only=(i == depth - 1))
    return xa
```

```python
import functools
import math

import jax
import jax.numpy as jnp
from jax import lax
from jax.experimental import pallas as pl
from jax.experimental.pallas import tpu as pltpu

RMS_EPS = 1e-6
ROPE_THETA = 10000.0
GRID_W = 64
POOL_WINDOWS = (2, 4, 8, 16)
N_EXPERTS = 32
TOP_K = 4
SWIGLU_LIMIT = 7.0
SWIGLU_ALPHA = 1.702
GQA_HEAD_DIM = 128
GQA_Q_HEADS = 8
GQA_KV_HEADS = 2
DIFF_HEAD_DIM = 64
DIFF_HEADS = 8
MOD_CHUNKS = 6

LANES = 128
SUBLANES = 8
VMEM_LIMIT_BYTES = 56 * 1024 * 1024

TM = 256
EXPERT_ROWS = 512
KV_CHUNK = 2048
SCORE_COLS = 256
LOG2E = math.log2(math.e)
HALO = SUBLANES

MXU_DTYPE = jnp.bfloat16
F32 = jnp.float32
NEG_BIG = -1e30


def _params(*sem):
    return pltpu.CompilerParams(dimension_semantics=sem, vmem_limit_bytes=VMEM_LIMIT_BYTES)


def _dot(a, b):
    return jnp.dot(a.astype(MXU_DTYPE), b.astype(MXU_DTYPE), preferred_element_type=F32)


def _rms(x, g):
    return x * lax.rsqrt(jnp.mean(x * x, axis=-1, keepdims=True) + RMS_EPS) * g


def _sigmoid(z):
    return 1.0 / (1.0 + jnp.exp(-z))


def _shift_up(a, s):
    n = a.shape[0]
    return pltpu.roll(a, (n - s) % n, 0)


def _mod_kernel(cv_ref, w_ref, b_ref, o_ref):
    a = cv_ref[...]
    o_ref[0] = _dot(a * _sigmoid(a), w_ref[0]) + b_ref[0]


def _modulation(cv, mod_w, mod_b):
    depth, d, n = mod_w.shape
    tn = n // 4
    return pl.pallas_call(
        _mod_kernel,
        grid=(depth, n // tn),
        in_specs=[pl.BlockSpec((SUBLANES, d), lambda i, k: (0, 0)),
                  pl.BlockSpec((1, d, tn), lambda i, k: (i, 0, k)),
                  pl.BlockSpec((1, 1, tn), lambda i, k: (i, 0, k))],
        out_specs=pl.BlockSpec((1, SUBLANES, tn), lambda i, k: (i, 0, k)),
        out_shape=jax.ShapeDtypeStruct((depth, SUBLANES, n), F32),
        compiler_params=_params("arbitrary", "arbitrary"),
        name="modulation",
    )(cv, mod_w, mod_b.reshape(depth, 1, n))


def _post_mixer(y, x, mod, ng, rwh_ref, rwl_ref, rb_ref, base_ref,
                x1_ref, h2_ref, mi_ref, mf_ref, cnt_ref):
    tm = x.shape[0]
    x1 = x + mod[2:3] * _rms(y, ng[1:2])
    h2 = _rms(x1, ng[2:3]) * (1.0 + mod[4:5]) + mod[3:4]
    x1_ref[0] = x1
    h2_ref[...] = h2

    h_hi = h2.astype(jnp.bfloat16)
    h_lo = (h2 - h_hi.astype(F32)).astype(jnp.bfloat16)
    rwh = rwh_ref[...]
    logits = (jnp.dot(h_hi, rwh, preferred_element_type=F32)
              + jnp.dot(h_lo, rwh, preferred_element_type=F32)
              + jnp.dot(h_hi, rwl_ref[...], preferred_element_type=F32)) + rb_ref[...]

    lane = lax.broadcasted_iota(jnp.int32, (tm, LANES), 1)
    lane_f = lane.astype(F32)
    vals, idxs, hots = [], [], []
    l = logits
    for _ in range(TOP_K):
        m = jnp.max(l, axis=-1, keepdims=True)
        idx = jnp.min(jnp.where(l == m, lane_f, float(LANES)), axis=-1, keepdims=True)
        hot = lane_f == idx
        vals.append(m)
        idxs.append(idx)
        hots.append(hot)
        l = jnp.where(hot, -3e38, l)
    exps = [jnp.exp(v - vals[0]) for v in vals]
    den = exps[0] + exps[1] + exps[2] + exps[3]

    @pl.when((pl.program_id(0) == 0) & (pl.program_id(1) == 0))
    def _():
        base_ref[...] = jnp.zeros_like(base_ref)

    hot_all = (hots[0].astype(F32) + hots[1].astype(F32) + hots[2].astype(F32) + hots[3].astype(F32))
    rows = lax.broadcasted_iota(jnp.int32, (tm, tm), 0)
    cols = lax.broadcasted_iota(jnp.int32, (tm, tm), 1)
    lower = (rows > cols).astype(jnp.bfloat16)
    before = jnp.dot(lower, hot_all.astype(jnp.bfloat16), preferred_element_type=F32) + base_ref[...]
    mi = jnp.zeros((tm, LANES), F32)
    mf = jnp.zeros((tm, LANES), F32)
    for k in range(TOP_K):
        rank = jnp.sum(jnp.where(hots[k], before, 0.0), axis=-1, keepdims=True)
        mi = mi + jnp.where(lane == k, idxs[k], 0.0) + jnp.where(lane == TOP_K + k, rank, 0.0)
        mf = mf + jnp.where(lane == k, exps[k] / den, 0.0)
    mi_ref[...] = mi.astype(jnp.int32)
    mf_ref[...] = mf
    base_ref[...] = base_ref[...] + jnp.sum(hot_all, axis=0, keepdims=True)
    cnt_ref[...] = base_ref[...]


def _mixer_call(body, x, modt, ng, router, extra_args, extra_specs, name):
    b, s, d = x.shape
    nt = s // TM
    rwh, rwl, rb = router
    n_extra = len(extra_args)

    def kern(*refs):
        x_ref, mod_ref, ng_ref, rwh_ref, rwl_ref, rb_ref = refs[:6]
        extra = refs[6:6 + n_extra]
        x1_ref, h2_ref, mi_ref, mf_ref, cnt_ref, base_ref = refs[6 + n_extra:]
        xt = x_ref[0]
        mod = mod_ref[0]
        ngv = ng_ref[...]
        y = body(xt, mod, ngv, *extra)
        _post_mixer(y, xt, mod, ngv, rwh_ref, rwl_ref, rb_ref, base_ref,
                    x1_ref, h2_ref, mi_ref, mf_ref, cnt_ref)

    in_specs = [
        pl.BlockSpec((1, TM, d), lambda bi, j: (bi, j, 0)),
        pl.BlockSpec((1, SUBLANES, d), lambda bi, j: (bi * 2 + jnp.minimum(j, 1), 0, 0)),
        pl.BlockSpec((4, d), lambda bi, j: (0, 0)),
        pl.BlockSpec((d, LANES), lambda bi, j: (0, 0)),
        pl.BlockSpec((d, LANES), lambda bi, j: (0, 0)),
        pl.BlockSpec((1, LANES), lambda bi, j: (0, 0)),
    ] + list(extra_specs)
    out_specs = [
        pl.BlockSpec((1, TM, d), lambda bi, j: (bi, j, 0)),
        pl.BlockSpec((TM, d), lambda bi, j: (bi * nt + j, 0)),
        pl.BlockSpec((TM, LANES), lambda bi, j: (bi * nt + j, 0)),
        pl.BlockSpec((TM, LANES), lambda bi, j: (bi * nt + j, 0)),
        pl.BlockSpec((1, LANES), lambda bi, j: (0, 0)),
    ]
    out_shape = [
        jax.ShapeDtypeStruct((b, s, d), F32),
        jax.ShapeDtypeStruct((b * s, d), F32),
        jax.ShapeDtypeStruct((b * s, LANES), jnp.int32),
        jax.ShapeDtypeStruct((b * s, LANES), F32),
        jax.ShapeDtypeStruct((1, LANES), F32),
    ]
    return pl.pallas_call(
        kern, grid=(b, nt), in_specs=in_specs, out_specs=out_specs, out_shape=out_shape,
        scratch_shapes=[pltpu.VMEM((1, LANES), F32)],
        compiler_params=_params("arbitrary", "arbitrary"), name=name,
    )(x, modt, ng, rwh, rwl, rb, *extra_args)


def _halo_specs(s, d):
    last = s // HALO - 1
    per_tile = TM // HALO
    prev = pl.BlockSpec((1, HALO, d), lambda bi, j: (bi, jnp.maximum(j * per_tile - 1, 0), 0))
    nxt = pl.BlockSpec((1, HALO, d), lambda bi, j: (bi, jnp.minimum((j + 1) * per_tile, last), 0))
    return prev, nxt


def _halo_valid():
    j = pl.program_id(1)
    nt = pl.num_programs(1)
    return j >= 2, (j >= 1) & (j <= nt - 2)


def _pool_body(ctx_len, lat_len, xt, mod, ng, xp_ref, xn_ref, pw_ref, ps_ref):
    tm, d = xt.shape
    group = d // len(POOL_WINDOWS)
    prev_ok, next_ok = _halo_valid()
    pre = lambda rows: _rms(rows, ng[0:1]) * (1.0 + mod[1:2]) + mod[0:1]
    hc = pre(xt)
    hp = jnp.where(prev_ok, pre(xp_ref[0]), 0.0)
    hn = jnp.where(next_ok, pre(xn_ref[0]), 0.0)
    ext = jnp.concatenate([hp, hc, hn], axis=0)

    j = pl.program_id(1)
    pos = lax.broadcasted_iota(jnp.int32, (tm, 1), 0) + jnp.where(j == 0, 0, (j - 1) * tm)
    seq_len = jnp.where(j == 0, ctx_len, lat_len)
    outs = []
    for g, w in enumerate(POOL_WINDOWS):
        e = ext[:, g * group:(g + 1) * group]
        run, span = e, 1
        while span < w:
            run = run + _shift_up(run, span)
            span *= 2
        win = _shift_up(run, HALO - w // 2)[:tm] if w // 2 != HALO else run[:tm]
        lo = jnp.maximum(pos - w // 2, 0)
        hi = jnp.minimum(pos + (w - w // 2), seq_len)
        diff = win / (hi - lo).astype(F32) - hc[:, g * group:(g + 1) * group]
        outs.append(_dot(diff, pw_ref[g]))
    return jnp.concatenate(outs, axis=1) * ps_ref[...]


def _pool_layer(x, modt, ng, router, pool_w, pool_scale, ctx_len):
    b, s, d = x.shape
    prev, nxt = _halo_specs(s, d)
    g = len(POOL_WINDOWS)
    specs = [prev, nxt,
             pl.BlockSpec((g, d // g, d // g), lambda bi, j: (0, 0, 0)),
             pl.BlockSpec((1, d), lambda bi, j: (0, 0))]
    body = functools.partial(_pool_body, ctx_len, s - ctx_len)
    return _mixer_call(body, x, modt, ng, router,
                       (x, x, pool_w.astype(MXU_DTYPE), pool_scale.reshape(1, d)), specs, "pool_mixer")


def _conv_body(xt, mod, ng, xp_ref, xn_ref, win_ref, wdw_ref, wout_ref):
    tm, d = xt.shape
    prev_ok, next_ok = _halo_valid()
    rows = jnp.concatenate([xp_ref[0], xt, xn_ref[0]], axis=0)
    h = _rms(rows, ng[0:1]) * (1.0 + mod[1:2]) + mod[0:1]
    proj = _dot(h, win_ref[...])
    gate_b = proj[HALO:HALO + tm, :d]
    u = proj[:, d:2 * d] * proj[:, 2 * d:]
    r = lax.broadcasted_iota(jnp.int32, (tm + 2 * HALO, 1), 0)
    keep = ((r >= HALO) | prev_ok) & ((r < HALO + tm) | next_ok)
    u = jnp.where(keep, u, 0.0)
    wdw = wdw_ref[...]
    conv = (wdw[0:1] * _shift_up(u, HALO - 1)[:tm] + wdw[1:2] * u[HALO:HALO + tm]
            + wdw[2:3] * _shift_up(u, HALO + 1)[:tm])
    return _dot(gate_b * conv, wout_ref[...])


def _conv_layer(x, modt, ng, router, w_in, w_dw, w_out):
    b, s, d = x.shape
    prev, nxt = _halo_specs(s, d)
    specs = [prev, nxt,
             pl.BlockSpec((d, 3 * d), lambda bi, j: (0, 0)),
             pl.BlockSpec((SUBLANES, d), lambda bi, j: (0, 0)),
             pl.BlockSpec((d, d), lambda bi, j: (0, 0))]
    w_dw8 = jnp.zeros((SUBLANES, d), F32).at[:w_dw.shape[0]].set(w_dw)
    return _mixer_call(_conv_body, x, modt, ng, router,
                       (x, x, w_in.astype(MXU_DTYPE), w_dw8, w_out.astype(MXU_DTYPE)), specs, "conv_mixer")


def _gqa_qkv_kernel(x_ref, mod_ref, ng_ref, w_ref, qg_ref, kg_ref, cos_ref, sin_ref,
                    q_ref, k_ref, v_ref):
    mod = mod_ref[0]
    h = _rms(x_ref[0], ng_ref[0:1]) * (1.0 + mod[1:2]) + mod[0:1]
    qkv = _dot(h, w_ref[...])
    cos, sin = cos_ref[...], sin_ref[...]
    hd = GQA_HEAD_DIM

    def rope(t):
        return t * cos + pltpu.roll(t, hd // 2, 1) * sin

    scale = hd ** -0.5 * LOG2E
    for i in range(GQA_Q_HEADS):
        q = rope(_rms(qkv[:, i * hd:(i + 1) * hd], qg_ref[...]))
        q_ref[0, i] = (q * scale).astype(q_ref.dtype)
    for i in range(GQA_KV_HEADS):
        o = (GQA_Q_HEADS + i) * hd
        k_ref[0, i] = rope(_rms(qkv[:, o:o + hd], kg_ref[...])).T.astype(k_ref.dtype)
        o = (GQA_Q_HEADS + GQA_KV_HEADS + i) * hd
        v_ref[0, i] = qkv[:, o:o + hd].astype(v_ref.dtype)


def _gqa_qkv(x, modt, ng, w_qkv, q_gain, k_gain, cos, sin):
    b, s, d = x.shape
    hd = GQA_HEAD_DIM
    n = w_qkv.shape[1]
    return pl.pallas_call(
        _gqa_qkv_kernel, grid=(b, s // TM),
        in_specs=[pl.BlockSpec((1, TM, d), lambda bi, j: (bi, j, 0)),
                  pl.BlockSpec((1, SUBLANES, d), lambda bi, j: (bi * 2 + jnp.minimum(j, 1), 0, 0)),
                  pl.BlockSpec((4, d), lambda bi, j: (0, 0)),
                  pl.BlockSpec((d, n), lambda bi, j: (0, 0)),
                  pl.BlockSpec((1, hd), lambda bi, j: (0, 0)),
                  pl.BlockSpec((1, hd), lambda bi, j: (0, 0)),
                  pl.BlockSpec((TM, hd), lambda bi, j: (j, 0)),
                  pl.BlockSpec((TM, hd), lambda bi, j: (j, 0))],
        out_specs=[pl.BlockSpec((1, GQA_Q_HEADS, TM, hd), lambda bi, j: (bi, 0, j, 0)),
                   pl.BlockSpec((1, GQA_KV_HEADS, hd, TM), lambda bi, j: (bi, 0, 0, j)),
                   pl.BlockSpec((1, GQA_KV_HEADS, TM, hd), lambda bi, j: (bi, 0, j, 0))],
        out_shape=[jax.ShapeDtypeStruct((b, GQA_Q_HEADS, s, hd), MXU_DTYPE),
                   jax.ShapeDtypeStruct((b, GQA_KV_HEADS, hd, s), MXU_DTYPE),
                   jax.ShapeDtypeStruct((b, GQA_KV_HEADS, s, hd), MXU_DTYPE)],
        compiler_params=_params("arbitrary", "arbitrary"), name="gqa_qkv",
    )(x, modt, ng, w_qkv.astype(MXU_DTYPE), q_gain.reshape(1, hd), k_gain.reshape(1, hd), cos, sin)


def _diff_qkv_kernel(x_ref, mod_ref, ng_ref, w_ref, cos_ref, sin_ref, q_ref, k_ref, v_ref):
    mod = mod_ref[0]
    d = x_ref.shape[2]
    h = _rms(x_ref[0], ng_ref[0:1]) * (1.0 + mod[1:2]) + mod[0:1]
    qkv = _dot(h, w_ref[...])
    cos, sin = cos_ref[...], sin_ref[...]
    hw = 2 * DIFF_HEAD_DIM
    lane = lax.broadcasted_iota(jnp.int32, (x_ref.shape[1], hw), 1)
    quarter = DIFF_HEAD_DIM // 2
    take_up = (lane // quarter) % 2 == 0

    def rope(t):
        rot = jnp.where(take_up, pltpu.roll(t, hw - quarter, 1), pltpu.roll(t, quarter, 1))
        return t * cos + rot * sin

    scale = DIFF_HEAD_DIM ** -0.5 * LOG2E
    first = lane < DIFF_HEAD_DIM
    for i in range(DIFF_HEADS):
        q = rope(qkv[:, i * hw:(i + 1) * hw]) * scale
        q_ref[0, 2 * i] = jnp.where(first, q, 0.0).astype(q_ref.dtype)
        q_ref[0, 2 * i + 1] = jnp.where(first, 0.0, q).astype(q_ref.dtype)
        k_ref[0, i] = rope(qkv[:, d + i * hw:d + (i + 1) * hw]).T.astype(k_ref.dtype)
        v_ref[0, i] = qkv[:, 2 * d + i * hw:2 * d + (i + 1) * hw].astype(v_ref.dtype)


def _diff_qkv(x, modt, ng, w_qkv, cos, sin):
    b, s, d = x.shape
    hw = 2 * DIFF_HEAD_DIM
    n = w_qkv.shape[1]
    return pl.pallas_call(
        _diff_qkv_kernel, grid=(b, s // TM),
        in_specs=[pl.BlockSpec((1, TM, d), lambda bi, j: (bi, j, 0)),
                  pl.BlockSpec((1, SUBLANES, d), lambda bi, j: (bi * 2 + jnp.minimum(j, 1), 0, 0)),
                  pl.BlockSpec((4, d), lambda bi, j: (0, 0)),
                  pl.BlockSpec((d, n), lambda bi, j: (0, 0)),
                  pl.BlockSpec((TM, hw), lambda bi, j: (j, 0)),
                  pl.BlockSpec((TM, hw), lambda bi, j: (j, 0))],
        out_specs=[pl.BlockSpec((1, 2 * DIFF_HEADS, TM, hw), lambda bi, j: (bi, 0, j, 0)),
                   pl.BlockSpec((1, DIFF_HEADS, hw, TM), lambda bi, j: (bi, 0, 0, j)),
                   pl.BlockSpec((1, DIFF_HEADS, TM, hw), lambda bi, j: (bi, 0, j, 0))],
        out_shape=[jax.ShapeDtypeStruct((b, 2 * DIFF_HEADS, s, hw), MXU_DTYPE),
                   jax.ShapeDtypeStruct((b, DIFF_HEADS, hw, s), MXU_DTYPE),
                   jax.ShapeDtypeStruct((b, DIFF_HEADS, s, hw), MXU_DTYPE)],
        compiler_params=_params("arbitrary", "arbitrary"), name="diff_qkv",
    )(x, modt, ng, w_qkv.astype(MXU_DTYPE), cos, sin)


def _flash_kernel(*refs, group, ctx_len, kv_chunk, n_chunks, lam_init):
    if lam_init is None:
        q_ref, kt_ref, v_ref, o_ref, s_even, s_odd = refs
    else:
        q_ref, kt_ref, v_ref, lam_ref, sg_ref, o_ref, s_even, s_odd = refs
    s_bufs = (s_even, s_odd)
    tq, hd = q_ref.shape[2], q_ref.shape[3]
    j = pl.program_id(2)
    rows = group * tq
    q = q_ref[0].reshape(rows, hd)
    n_sub = kv_chunk // SCORE_COLS

    def qk(start):
        return jnp.dot(q, kt_ref[0, 0, :, pl.ds(start, SCORE_COLS)], preferred_element_type=F32)

    def probs(s, m):
        return jnp.exp2((s - m).astype(v_ref.dtype))

    def pv(p, start):
        v = v_ref[0, 0, pl.ds(start, SCORE_COLS), :]
        return jnp.dot(p, jnp.concatenate([v, jnp.ones_like(v)], axis=1), preferred_element_type=F32)

    def fold(x):
        return [x[:, c * LANES:(c + 1) * LANES] for c in range(SCORE_COLS // LANES)]

    def chunk_start(i):
        start = ctx_len + i * kv_chunk
        return start if isinstance(i, int) else pl.multiple_of(start, SCORE_COLS)

    s = qk(0)
    m = jnp.max(s, axis=-1, keepdims=True)
    acc = pv(probs(s, m), 0)

    def latent_keys(state):
        def scores_and_max(i, buf, c, mx):
            sc = qk(chunk_start(i) + c * SCORE_COLS)
            buf[:, c * SCORE_COLS:(c + 1) * SCORE_COLS] = sc
            for part in fold(sc):
                mx = part if mx is None else jnp.maximum(mx, part)
            return mx

        def new_max(m, mx):
            m_new = jnp.maximum(m, jnp.max(mx, axis=-1, keepdims=True))
            return m_new, jnp.exp2(m - m_new)

        def step(i, parity, has_next, m, alpha, acc):
            acc = alpha * acc
            mx = None
            for c in range(n_sub):
                if has_next:
                    mx = scores_and_max(i + 1, s_bufs[1 - parity], c, mx)
                pc = probs(s_bufs[parity][:, c * SCORE_COLS:(c + 1) * SCORE_COLS], m)
                acc = acc + pv(pc, chunk_start(i) + c * SCORE_COLS)
            if has_next:
                m, alpha = new_max(m, mx)
            return m, alpha, acc

        m, acc = state
        mx = None
        for c in range(n_sub):
            mx = scores_and_max(0, s_bufs[0], c, mx)
        m, alpha = new_max(m, mx)
        pairs = (n_chunks - 1) // 2

        def body(t, carry):
            carry = step(2 * t, 0, True, *carry)
            return step(2 * t + 1, 1, True, *carry)

        carry = lax.fori_loop(0, pairs, body, (m, alpha, acc))
        for i in range(2 * pairs, n_chunks):
            carry = step(i, i % 2, i < n_chunks - 1, *carry)
        return carry[0], carry[2]

    _, acc = lax.cond(j > 0, latent_keys, lambda state: state, (m, acc))
    o = acc[:, :hd] / acc[:, hd:]
    if lam_init is None:
        for g in range(group):
            o_ref[0, :, g * hd:(g + 1) * hd] = o[g * tq:(g + 1) * tq].astype(o_ref.dtype)
    else:
        lv = lam_ref[...]
        lam = (jnp.exp(jnp.sum(lv[0:1] * lv[1:2], axis=-1, keepdims=True))
               - jnp.exp(jnp.sum(lv[2:3] * lv[3:4], axis=-1, keepdims=True)) + lam_init)
        od = o[:tq] - lam * o[tq:]
        o_ref[0] = (_rms(od, sg_ref[...]) * (1.0 - lam_init)).astype(o_ref.dtype)


def _flash(q, k, v, group, ctx_len, lam=None, subln=None, lam_init=None):
    b, hq, s, hd = q.shape
    hkv = k.shape[1]
    lat = s - ctx_len
    kv_chunk = min(KV_CHUNK, lat)
    assert ctx_len == SCORE_COLS and kv_chunk % SCORE_COLS == 0 and lat % kv_chunk == 0
    kern =functools.partial(_flash_kernel, group=group, ctx_len=ctx_len, kv_chunk=kv_chunk,
                             n_chunks=lat // kv_chunk, lam_init=lam_init)
    in_specs = [pl.BlockSpec((1, group, TM, hd), lambda bi, h, j: (bi, h, j, 0)),
                pl.BlockSpec((1, 1, hd, s), lambda bi, h, j: (bi, h, 0, 0)),
                pl.BlockSpec((1, 1, s, hd), lambda bi, h, j: (bi, h, 0, 0))]
    args = [q, k, v]
    if lam_init is None:
        out_w = group * hd
    else:
        out_w = hd
        in_specs += [pl.BlockSpec(lam.shape, lambda bi, h, j: (0, 0)),
                     pl.BlockSpec((1, hd), lambda bi, h, j: (0, 0))]
        args += [lam, subln.reshape(1, hd)]
    return pl.pallas_call(
        kern, grid=(b, hkv, s // TM), in_specs=in_specs,
        out_specs=pl.BlockSpec((1, TM, out_w), lambda bi, h, j: (bi, j, h)),
        out_shape=jax.ShapeDtypeStruct((b, s, hkv * out_w), MXU_DTYPE),
        scratch_shapes=[pltpu.VMEM((group * TM, kv_chunk), F32), pltpu.VMEM((group * TM, kv_chunk), F32)],
        compiler_params=_params("arbitrary", "arbitrary", "arbitrary"), name="flash_attention",
    )(*args)


def _attn_out_body(xt, mod, ng, o_ref, wo_ref):
    return _dot(o_ref[0], wo_ref[...])


def _attn_out_layer(x, modt, ng, router, o, w_o):
    b, s, d = x.shape
    specs = [pl.BlockSpec((1, TM, d), lambda bi, j: (bi, j, 0)),
             pl.BlockSpec((d, d), lambda bi, j: (0, 0))]
    return _mixer_call(_attn_out_body, x, modt, ng, router, (o, w_o.astype(MXU_DTYPE)), specs, "attn_out")


def _rope_tables(ctx_len, lat_len, dim, reps):
    n = dim // 4
    pos = jnp.arange(lat_len, dtype=jnp.int32)
    row = (pos // GRID_W).astype(F32)
    col = (pos % GRID_W).astype(F32)
    freqs = ROPE_THETA ** (-jnp.arange(n, dtype=F32) / n)
    ang = jnp.concatenate([row[:, None] * freqs, col[:, None] * freqs], axis=-1)
    ang = jnp.concatenate([jnp.zeros((ctx_len, dim // 2), F32), ang], axis=0)
    cos, sin = jnp.cos(ang), jnp.sin(ang)
    return jnp.tile(cos, (1, 2 * reps)), jnp.tile(jnp.concatenate([-sin, sin], axis=-1), (1, reps))


def _dispatch_kernel(zero_ref, dest_ref, h_ref, hs_hbm, zero_buf, sem):
    @pl.when(pl.program_id(0) == 0)
    def _():
        zero_buf[...] = jnp.zeros_like(zero_buf)

        def zero_copy(e):
            start = pl.multiple_of(zero_ref[e], SUBLANES)
            return pltpu.make_async_copy(zero_buf, hs_hbm.at[pl.ds(start, EXPERT_ROWS)], sem)

        for e in range(2 * N_EXPERTS):
            zero_copy(e).start()
        for e in range(2 * N_EXPERTS):
            zero_copy(e).wait()

    for r in range(TM):
        for k in range(TOP_K):
            pltpu.make_async_copy(h_ref.at[pl.ds(r, 1)], hs_hbm.at[pl.ds(dest_ref[0, 0, r * TOP_K + k], 1)],
                                  sem).start()
    for _ in range(TOP_K):
        pltpu.make_async_copy(h_ref, h_ref, sem).wait()


def _dispatch(h2, dest3, zero_start, n_pad):
    t, d = h2.shape
    grid_spec = pltpu.PrefetchScalarGridSpec(
        num_scalar_prefetch=1, grid=(t // TM,),
        in_specs=[pl.BlockSpec((1, 1, TM * TOP_K), lambda i, zs: (i, 0, 0), memory_space=pltpu.SMEM),
                  pl.BlockSpec((TM, d), lambda i, zs: (i, 0))],
        out_specs=pl.BlockSpec(memory_space=pl.ANY),
        scratch_shapes=[pltpu.VMEM((EXPERT_ROWS, d), F32), pltpu.SemaphoreType.DMA])
    return pl.pallas_call(
        _dispatch_kernel, grid_spec=grid_spec,
        out_shape=jax.ShapeDtypeStruct((n_pad, d), F32),
        compiler_params=pltpu.CompilerParams(dimension_semantics=("arbitrary",), has_side_effects=True),
        name="moe_dispatch",
    )(zero_start, dest3, h2)


def _expert_kernel(be_ref, nused_ref, hs_ref, w1_ref, b1_ref, w2_ref, b2_ref, o_ref, w1_mxu, w2_mxu):
    i = pl.program_id(0)
    f = w2_ref.shape[1]

    @pl.when((i == 0) | (be_ref[i] != be_ref[jnp.maximum(i - 1, 0)]))
    def _():
        w1_mxu[...] = w1_ref[0].astype(w1_mxu.dtype)
        w2_mxu[...] = w2_ref[0].astype(w2_mxu.dtype)

    @pl.when(i < nused_ref[0])
    def _():
        gu = _dot(hs_ref[...], w1_mxu[...]) + b1_ref[0]
        g = jnp.minimum(gu[:, :f], SWIGLU_LIMIT)
        u = jnp.clip(gu[:, f:], -SWIGLU_LIMIT, SWIGLU_LIMIT)
        act = g * _sigmoid(SWIGLU_ALPHA * g) * (u + 1.0)
        o_ref[...] = _dot(act, w2_mxu[...]) + b2_ref[0]

    @pl.when(i >= nused_ref[0])
    def _():
        o_ref[...] = jnp.zeros_like(o_ref)


def _experts(hs, block_e, n_used, w1, b1, w2, b2, layer):
    n_pad, d = hs.shape
    _, e, _, f2 = w1.shape
    f = w2.shape[2]
    tb = EXPERT_ROWS
    grid_spec = pltpu.PrefetchScalarGridSpec(
        num_scalar_prefetch=2, grid=(n_pad // tb,),
        in_specs=[pl.BlockSpec((tb, d), lambda i, be, nu: (i, 0)),
                  pl.BlockSpec((None, 1, d, f2), lambda i, be, nu: (layer, be[i], 0, 0)),
                  pl.BlockSpec((1, 1, f2), lambda i, be, nu: (be[i], 0, 0)),
                  pl.BlockSpec((None, 1, f, d), lambda i, be, nu: (layer, be[i], 0, 0)),
                  pl.BlockSpec((1, 1, d), lambda i, be, nu: (be[i], 0, 0))],
        out_specs=pl.BlockSpec((tb, d), lambda i, be, nu: (i, 0)),
        scratch_shapes=[pltpu.VMEM((d, f2), MXU_DTYPE), pltpu.VMEM((f, d), MXU_DTYPE)])
    return pl.pallas_call(
        _expert_kernel, grid_spec=grid_spec,
        out_shape=jax.ShapeDtypeStruct((n_pad, d), F32),
        compiler_params=_params("arbitrary"), name="moe_experts",
    )(block_e, n_used, hs, w1, b1.reshape(e, 1, f2), w2, b2.reshape(e, 1, d))


def _combine_kernel(dest_ref, dest_next_ref, ys_hbm, mf_ref, x1_ref, mod_ref, ng_ref, x2_ref, buf, sem):
    tile = pl.program_id(0) * pl.num_programs(1) + pl.program_id(1)
    n_tiles = pl.num_programs(0) * pl.num_programs(1)
    slot = tile % 2

    def fetch(idx_ref, to_slot):
        for r in range(TM):
            for k in range(TOP_K):
                pltpu.make_async_copy(ys_hbm.at[pl.ds(idx_ref[0, 0, r * TOP_K + k], 1)],
                                      buf.at[to_slot, k, pl.ds(r, 1)], sem.at[to_slot]).start()

    @pl.when(tile == 0)
    def _():
        fetch(dest_ref, 0)

    @pl.when(tile + 1 < n_tiles)
    def _():
        fetch(dest_next_ref, 1 - slot)

    pltpu.make_async_copy(buf.at[slot], buf.at[slot], sem.at[slot]).wait()
    gates = mf_ref[...]
    y = gates[:, 0:1] * buf[slot, 0]
    for k in range(1, TOP_K):
        y = y + gates[:, k:k + 1] * buf[slot, k]
    mod = mod_ref[0]
    x2_ref[0] = x1_ref[0] + mod[5:6] * _rms(y, ng_ref[3:4])


def _combine(ys, dest3, mf, x1, modt, ng, latent_only):
    b, s, d = x1.shape
    nt = s // TM
    if latent_only:
        out_spec = pl.BlockSpec((1, TM, d), lambda bi, j: (bi, jnp.maximum(j - 1, 0), 0))
        out_shape = jax.ShapeDtypeStruct((b, s - TM, d), F32)
    else:
        out_spec = pl.BlockSpec((1, TM, d), lambda bi, j: (bi, j, 0))
        out_shape = jax.ShapeDtypeStruct((b, s, d), F32)
    return pl.pallas_call(
        _combine_kernel, grid=(b, nt),
        in_specs=[pl.BlockSpec((1, 1, TM * TOP_K), lambda bi, j: (bi * nt + j, 0, 0), memory_space=pltpu.SMEM),
                  pl.BlockSpec((1, 1, TM * TOP_K), lambda bi, j: (jnp.minimum(bi * nt + j + 1, b * nt - 1), 0, 0),
                               memory_space=pltpu.SMEM),
                  pl.BlockSpec(memory_space=pl.ANY),
                  pl.BlockSpec((TM, LANES), lambda bi, j: (bi * nt + j, 0)),
                  pl.BlockSpec((1, TM, d), lambda bi, j: (bi, j, 0)),
                  pl.BlockSpec((1, SUBLANES, d), lambda bi, j: (bi * 2 + jnp.minimum(j, 1), 0, 0)),
                  pl.BlockSpec((4, d), lambda bi, j: (0, 0))],
        out_specs=out_spec, out_shape=out_shape,
        scratch_shapes=[pltpu.VMEM((2, TOP_K, TM, d), F32), pltpu.SemaphoreType.DMA((2,))],
        compiler_params=_params("arbitrary", "arbitrary"), name="moe_combine",
    )(dest3, dest3, ys, mf, x1, modt, ng)


def _moe(x1, h2, mi, mf, cnt, modt, ng, w1, b1, w2, b2, layer, latent_only):
    t = h2.shape[0]
    tb = EXPERT_ROWS
    counts = cnt[0, :N_EXPERTS].astype(jnp.int32)
    padded = (counts + tb - 1) // tb * tb
    pend = jnp.cumsum(padded)
    pstart = pend - padded
    dest = pstart[mi[:, :TOP_K]] + mi[:, TOP_K:2 * TOP_K]
    n_blocks = -(-(t * TOP_K) // tb) + N_EXPERTS
    block_row0 = jnp.arange(n_blocks, dtype=jnp.int32) * tb
    block_e = jnp.minimum(jnp.sum((pend[None, :] <= block_row0[:, None]).astype(jnp.int32), axis=1),
                          N_EXPERTS - 1)
    n_used = (pend[-1:] // tb).astype(jnp.int32)
    dest3 = dest.reshape(t // TM, 1, TM * TOP_K)
    tail_start = jnp.maximum(pend - tb, 0).astype(jnp.int32)
    trailing = block_row0[n_blocks - N_EXPERTS:]
    zero_start = jnp.concatenate([tail_start, jnp.where(trailing >= pend[-1], trailing, tail_start[0])])
    hs = _dispatch(h2, dest3, zero_start, n_blocks * tb)
    ys = _experts(hs, block_e, n_used, w1, b1, w2, b2, layer)
    return _combine(ys, dest3, mf, x1, modt, ng, latent_only)


def kernel(x, c, ctx, c_ctx, mod_w, mod_b, norm_g, pool_w, pool_scale, gqa_w_qkv, gqa_w_o, gqa_q_gain, gqa_k_gain, conv_w_in, conv_w_dw, conv_w_out, diff_w_qkv, diff_w_o, diff_lambda, diff_subln_gain, router_w, router_b, moe_w1, moe_b1, moe_w2, moe_b2):
    b, lat_len, d = x.shape
    ctx_len = ctx.shape[1]
    depth = mod_w.shape[0]
    n_mixers = 4
    assert ctx_len == TM and lat_len % TM == 0 and b + 1 <= SUBLANES and d % LANES == 0

    xa = jnp.concatenate([ctx, x], axis=1)

    cv = jnp.zeros((SUBLANES, d), F32).at[:b].set(c).at[b].set(c_ctx)
    mods = _modulation(cv, mod_w, mod_b).reshape(depth, SUBLANES, MOD_CHUNKS, d)
    modt_all = jnp.stack([jnp.broadcast_to(mods[:, b][:, None], (depth, b, MOD_CHUNKS, d)), mods[:, :b]], axis=2)
    modt_all = jnp.pad(modt_all, ((0, 0), (0, 0), (0, 0), (0, SUBLANES - MOD_CHUNKS), (0, 0)))
    modt_all = modt_all.reshape(depth, b * 2, SUBLANES, d)

    cos_b, sin_b = _rope_tables(ctx_len, lat_len, GQA_HEAD_DIM, 1)
    cos_d, sin_d = _rope_tables(ctx_len, lat_len, DIFF_HEAD_DIM, 2)

    for i in range(depth):
        m, jj = i % n_mixers, i // n_mixers
        modt, ng = modt_all[i], norm_g[i]
        rw = jnp.zeros((d, LANES), F32).at[:, :N_EXPERTS].set(router_w[i])
        rwh = rw.astype(jnp.bfloat16)
        rwl = (rw - rwh.astype(F32)).astype(jnp.bfloat16)
        rb = jnp.full((1, LANES), NEG_BIG, F32).at[0, :N_EXPERTS].set(router_b[i])
        router = (rwh, rwl, rb)
        if m == 0:
            res = _pool_layer(xa, modt, ng, router, pool_w[jj], pool_scale[jj], ctx_len)
        elif m == 1:
            q, k, v = _gqa_qkv(xa, modt, ng, gqa_w_qkv[jj], gqa_q_gain[jj], gqa_k_gain[jj], cos_b, sin_b)
            o = _flash(q, k, v, GQA_Q_HEADS // GQA_KV_HEADS, ctx_len)
            res = _attn_out_layer(xa, modt, ng, router, o, gqa_w_o[jj])
        elif m == 2:
            res = _conv_layer(xa, modt, ng, router, conv_w_in[jj], conv_w_dw[jj], conv_w_out[jj])
        else:
            lam_init = 0.8 - 0.6 * math.exp(-0.3 * i)
            q, k, v = _diff_qkv(xa, modt, ng, diff_w_qkv[jj], cos_d, sin_d)
            o = _flash(q, k, v, 2, ctx_len, lam=diff_lambda[jj], subln=diff_subln_gain[jj], lam_init=lam_init)
            res = _attn_out_layer(xa, modt, ng, router, o, diff_w_o[jj])
        x1, h2, mi, mf, cnt = res
        xa = _moe(x1, h2, mi, mf, cnt, modt, ng,
                  moe_w1, moe_b1[i], moe_w2, moe_b2[i], layer=i, latent_only=(i == depth - 1))
    return xa
```

```python
import functools
import math

import jax
import jax.numpy as jnp
from jax import lax
from jax.experimental import pallas as pl
from jax.experimental.pallas import tpu as pltpu

RMS_EPS = 1e-6
ROPE_THETA = 10000.0
GRID_W = 64
POOL_WINDOWS = (2, 4, 8, 16)
N_EXPERTS = 32
TOP_K = 4
SWIGLU_LIMIT = 7.0
SWIGLU_ALPHA = 1.702
GQA_HEAD_DIM = 128
GQA_Q_HEADS = 8
GQA_KV_HEADS = 2
DIFF_HEAD_DIM = 64
DIFF_HEADS = 8
MOD_CHUNKS = 6

LANES = 128
SUBLANES = 8
VMEM_LIMIT_BYTES = 56 * 1024 * 1024

TM = 256
EXPERT_ROWS = 512
KV_CHUNK = 2048
SCORE_COLS = 256
LOG2E = math.log2(math.e)
HALO = SUBLANES

MXU_DTYPE = jnp.bfloat16
F32 = jnp.float32
NEG_BIG = -1e30


def _params(*sem):
    return pltpu.CompilerParams(dimension_semantics=sem, vmem_limit_bytes=VMEM_LIMIT_BYTES)


def _dot(a, b):
    return jnp.dot(a.astype(MXU_DTYPE), b.astype(MXU_DTYPE), preferred_element_type=F32)


def _rms(x, g):
    return x * lax.rsqrt(jnp.mean(x * x, axis=-1, keepdims=True) + RMS_EPS) * g


def _sigmoid(z):
    return 1.0 / (1.0 + jnp.exp(-z))


def _shift_up(a, s):
    n = a.shape[0]
    return pltpu.roll(a, (n - s) % n, 0)


def _mod_kernel(cv_ref, w_ref, b_ref, o_ref):
    a = cv_ref[...]
    o_ref[0] = _dot(a * _sigmoid(a), w_ref[0]) + b_ref[0]


def _modulation(cv, mod_w, mod_b):
    depth, d, n = mod_w.shape
    tn = n // 4
    return pl.pallas_call(
        _mod_kernel,
        grid=(depth, n // tn),
        in_specs=[pl.BlockSpec((SUBLANES, d), lambda i, k: (0, 0)),
                  pl.BlockSpec((1, d, tn), lambda i, k: (i, 0, k)),
                  pl.BlockSpec((1, 1, tn), lambda i, k: (i, 0, k))],
        out_specs=pl.BlockSpec((1, SUBLANES, tn), lambda i, k: (i, 0, k)),
        out_shape=jax.ShapeDtypeStruct((depth, SUBLANES, n), F32),
        compiler_params=_params("arbitrary", "arbitrary"),
        name="modulation",
    )(cv, mod_w, mod_b.reshape(depth, 1, n))


def _post_mixer(y, x, mod, ng, rwh_ref, rwl_ref, rb_ref, base_ref,
                x1_ref, h2_ref, mi_ref, mf_ref, cnt_ref):
    tm = x.shape[0]
    x1 = x + mod[2:3] * _rms(y, ng[1:2])
    h2 = _rms(x1, ng[2:3]) * (1.0 + mod[4:5]) + mod[3:4]
    x1_ref[0] = x1
    h2_ref[...] = h2

    h_hi = h2.astype(jnp.bfloat16)
    h_lo = (h2 - h_hi.astype(F32)).astype(jnp.bfloat16)
    rwh = rwh_ref[...]
    logits = (jnp.dot(h_hi, rwh, preferred_element_type=F32)
              + jnp.dot(h_lo, rwh, preferred_element_type=F32)
              + jnp.dot(h_hi, rwl_ref[...], preferred_element_type=F32)) + rb_ref[...]

    lane = lax.broadcasted_iota(jnp.int32, (tm, LANES), 1)
    lane_f = lane.astype(F32)
    vals, idxs, hots = [], [], []
    l = logits
    for _ in range(TOP_K):
        m = jnp.max(l, axis=-1, keepdims=True)
        idx = jnp.min(jnp.where(l == m, lane_f, float(LANES)), axis=-1, keepdims=True)
        hot = lane_f == idx
        vals.append(m)
        idxs.append(idx)
        hots.append(hot)
        l = jnp.where(hot, -3e38, l)
    exps = [jnp.exp(v - vals[0]) for v in vals]
    den = exps[0] + exps[1] + exps[2] + exps[3]

    @pl.when((pl.program_id(0) == 0) & (pl.program_id(1) == 0))
    def _():
        base_ref[...] = jnp.zeros_like(base_ref)

    hot_all = (hots[0].astype(F32) + hots[1].astype(F32) + hots[2].astype(F32) + hots[3].astype(F32))
    rows = lax.broadcasted_iota(jnp.int32, (tm, tm), 0)
    cols = lax.broadcasted_iota(jnp.int32, (tm, tm), 1)
    lower = (rows > cols).astype(jnp.bfloat16)
    before = jnp.dot(lower, hot_all.astype(jnp.bfloat16), preferred_element_type=F32) + base_ref[...]
    mi = jnp.zeros((tm, LANES), F32)
    mf = jnp.zeros((tm, LANES), F32)
    for k in range(TOP_K):
        rank = jnp.sum(jnp.where(hots[k], before, 0.0), axis=-1, keepdims=True)
        mi = mi + jnp.where(lane == k, idxs[k], 0.0) + jnp.where(lane == TOP_K + k, rank, 0.0)
        mf = mf + jnp.where(lane == k, exps[k] / den, 0.0)
    mi_ref[...] = mi.astype(jnp.int32)
    mf_ref[...] = mf
    base_ref[...] = base_ref[...] + jnp.sum(hot_all, axis=0, keepdims=True)
    cnt_ref[...] = base_ref[...]


def _mixer_call(body, x, modt, ng, router, extra_args, extra_specs, name):
    b, s, d = x.shape
    nt = s // TM
    rwh, rwl, rb = router
    n_extra = len(extra_args)

    def kern(*refs):
        x_ref, mod_ref, ng_ref, rwh_ref, rwl_ref, rb_ref = refs[:6]
        extra = refs[6:6 + n_extra]
        x1_ref, h2_ref, mi_ref, mf_ref, cnt_ref, base_ref = refs[6 + n_extra:]
        xt = x_ref[0]
        mod = mod_ref[0]
        ngv = ng_ref[...]
        y = body(xt, mod, ngv, *extra)
        _post_mixer(y, xt, mod, ngv, rwh_ref, rwl_ref, rb_ref, base_ref,
                    x1_ref, h2_ref, mi_ref, mf_ref, cnt_ref)

    in_specs = [
        pl.BlockSpec((1, TM, d), lambda bi, j: (bi, j, 0)),
        pl.BlockSpec((1, SUBLANES, d), lambda bi, j: (bi * 2 + jnp.minimum(j, 1), 0, 0)),
        pl.BlockSpec((4, d), lambda bi, j: (0, 0)),
        pl.BlockSpec((d, LANES), lambda bi, j: (0, 0)),
        pl.BlockSpec((d, LANES), lambda bi, j: (0, 0)),
        pl.BlockSpec((1, LANES), lambda bi, j: (0, 0)),
    ] + list(extra_specs)
    out_specs = [
        pl.BlockSpec((1, TM, d), lambda bi, j: (bi, j, 0)),
        pl.BlockSpec((TM, d), lambda bi, j: (bi * nt + j, 0)),
        pl.BlockSpec((TM, LANES), lambda bi, j: (bi * nt + j, 0)),
        pl.BlockSpec((TM, LANES), lambda bi, j: (bi * nt + j, 0)),
        pl.BlockSpec((1, LANES), lambda bi, j: (0, 0)),
    ]
    out_shape = [
        jax.ShapeDtypeStruct((b, s, d), F32),
        jax.ShapeDtypeStruct((b * s, d), F32),
        jax.ShapeDtypeStruct((b * s, LANES), jnp.int32),
        jax.ShapeDtypeStruct((b * s, LANES), F32),
        jax.ShapeDtypeStruct((1, LANES), F32),
    ]
    return pl.pallas_call(
        kern, grid=(b, nt), in_specs=in_specs, out_specs=out_specs, out_shape=out_shape,
        scratch_shapes=[pltpu.VMEM((1, LANES), F32)],
        compiler_params=_params("arbitrary", "arbitrary"), name=name,
    )(x, modt, ng, rwh, rwl, rb, *extra_args)


def _halo_specs(s, d):
    last = s // HALO - 1
    per_tile = TM // HALO
    prev = pl.BlockSpec((1, HALO, d), lambda bi, j: (bi, jnp.maximum(j * per_tile - 1, 0), 0))
    nxt = pl.BlockSpec((1, HALO, d), lambda bi, j: (bi, jnp.minimum((j + 1) * per_tile, last), 0))
    return prev, nxt


def _halo_valid():
    j = pl.program_id(1)
    nt = pl.num_programs(1)
    return j >= 2, (j >= 1) & (j <= nt - 2)


def _pool_body(ctx_len, lat_len, xt, mod, ng, xp_ref, xn_ref, pw_ref, ps_ref):
    tm, d = xt.shape
    group = d // len(POOL_WINDOWS)
    prev_ok, next_ok = _halo_valid()
    pre = lambda rows: _rms(rows, ng[0:1]) * (1.0 + mod[1:2]) + mod[0:1]
    hc = pre(xt)
    hp = jnp.where(prev_ok, pre(xp_ref[0]), 0.0)
    hn = jnp.where(next_ok, pre(xn_ref[0]), 0.0)
    ext = jnp.concatenate([hp, hc, hn], axis=0)

    j = pl.program_id(1)
    pos = lax.broadcasted_iota(jnp.int32, (tm, 1), 0) + jnp.where(j == 0, 0, (j - 1) * tm)
    seq_len = jnp.where(j == 0, ctx_len, lat_len)
    outs = []
    for g, w in enumerate(POOL_WINDOWS):
        e = ext[:, g * group:(g + 1) * group]
        run, span = e, 1
        while span < w:
            run = run + _shift_up(run, span)
            span *= 2
        win = _shift_up(run, HALO - w // 2)[:tm] if w // 2 != HALO else run[:tm]
        lo = jnp.maximum(pos - w // 2, 0)
        hi = jnp.minimum(pos + (w - w // 2), seq_len)
        diff = win / (hi - lo).astype(F32) - hc[:, g * group:(g + 1) * group]
        outs.append(_dot(diff, pw_ref[g]))
    return jnp.concatenate(outs, axis=1) * ps_ref[...]


def _pool_layer(x, modt, ng, router, pool_w, pool_scale, ctx_len):
    b, s, d = x.shape
    prev, nxt = _halo_specs(s, d)
    g = len(POOL_WINDOWS)
    specs = [prev, nxt,
             pl.BlockSpec((g, d // g, d // g), lambda bi, j: (0, 0, 0)),
             pl.BlockSpec((1, d), lambda bi, j: (0, 0))]
    body = functools.partial(_pool_body, ctx_len, s - ctx_len)
    return _mixer_call(body, x, modt, ng, router,
                       (x, x, pool_w.astype(MXU_DTYPE), pool_scale.reshape(1, d)), specs, "pool_mixer")


def _conv_body(xt, mod, ng, xp_ref, xn_ref, win_ref, wdw_ref, wout_ref):
    tm, d = xt.shape
    prev_ok, next_ok = _halo_valid()
    rows = jnp.concatenate([xp_ref[0], xt, xn_ref[0]], axis=0)
    h = _rms(rows, ng[0:1]) * (1.0 + mod[1:2]) + mod[0:1]
    proj = _dot(h, win_ref[...])
    gate_b = proj[HALO:HALO + tm, :d]
    u = proj[:, d:2 * d] * proj[:, 2 * d:]
    r = lax.broadcasted_iota(jnp.int32, (tm + 2 * HALO, 1), 0)
    keep = ((r >= HALO) | prev_ok) & ((r < HALO + tm) | next_ok)
    u = jnp.where(keep, u, 0.0)
    wdw = wdw_ref[...]
    conv = (wdw[0:1] * _shift_up(u, HALO - 1)[:tm] + wdw[1:2] * u[HALO:HALO + tm]
            + wdw[2:3] * _shift_up(u, HALO + 1)[:tm])
    return _dot(gate_b * conv, wout_ref[...])


def _conv_layer(x, modt, ng, router, w_in, w_dw, w_out):
    b, s, d = x.shape
    prev, nxt = _halo_specs(s, d)
    specs = [prev, nxt,
             pl.BlockSpec((d, 3 * d), lambda bi, j: (0, 0)),
             pl.BlockSpec((SUBLANES, d), lambda bi, j: (0, 0)),
             pl.BlockSpec((d, d), lambda bi, j: (0, 0))]
    w_dw8 = jnp.zeros((SUBLANES, d), F32).at[:w_dw.shape[0]].set(w_dw)
    return _mixer_call(_conv_body, x, modt, ng, router,
                       (x, x, w_in.astype(MXU_DTYPE), w_dw8, w_out.astype(MXU_DTYPE)), specs, "conv_mixer")


def _gqa_qkv_kernel(x_ref, mod_ref, ng_ref, w_ref, qg_ref, kg_ref, cos_ref, sin_ref,
                    q_ref, k_ref, v_ref):
    mod = mod_ref[0]
    h = _rms(x_ref[0], ng_ref[0:1]) * (1.0 + mod[1:2]) + mod[0:1]
    qkv = _dot(h, w_ref[...])
    cos, sin = cos_ref[...], sin_ref[...]
    hd = GQA_HEAD_DIM

    def rope(t):
        return t * cos + pltpu.roll(t, hd // 2, 1) * sin

    scale = hd ** -0.5 * LOG2E
    for i in range(GQA_Q_HEADS):
        q = rope(_rms(qkv[:, i * hd:(i + 1) * hd], qg_ref[...]))
        q_ref[0, i] = (q * scale).astype(q_ref.dtype)
    for i in range(GQA_KV_HEADS):
        o = (GQA_Q_HEADS + i) * hd
        k_ref[0, i] = rope(_rms(qkv[:, o:o + hd], kg_ref[...])).T.astype(k_ref.dtype)
        o = (GQA_Q_HEADS + GQA_KV_HEADS + i) * hd
        v_ref[0, i] = qkv[:, o:o + hd].astype(v_ref.dtype)


def _gqa_qkv(x, modt, ng, w_qkv, q_gain, k_gain, cos, sin):
    b, s, d = x.shape
    hd = GQA_HEAD_DIM
    n = w_qkv.shape[1]
    return pl.pallas_call(
        _gqa_qkv_kernel, grid=(b, s // TM),
        in_specs=[pl.BlockSpec((1, TM, d), lambda bi, j: (bi, j, 0)),
                  pl.BlockSpec((1, SUBLANES, d), lambda bi, j: (bi * 2 + jnp.minimum(j, 1), 0, 0)),
                  pl.BlockSpec((4, d), lambda bi, j: (0, 0)),
                  pl.BlockSpec((d, n), lambda bi, j: (0, 0)),
                  pl.BlockSpec((1, hd), lambda bi, j: (0, 0)),
                  pl.BlockSpec((1, hd), lambda bi, j: (0, 0)),
                  pl.BlockSpec((TM, hd), lambda bi, j: (j, 0)),
                  pl.BlockSpec((TM, hd), lambda bi, j: (j, 0))],
        out_specs=[pl.BlockSpec((1, GQA_Q_HEADS, TM, hd), lambda bi, j: (bi, 0, j, 0)),
                   pl.BlockSpec((1, GQA_KV_HEADS, hd, TM), lambda bi, j: (bi, 0, 0, j)),
                   pl.BlockSpec((1, GQA_KV_HEADS, TM, hd), lambda bi, j: (bi, 0, j, 0))],
        out_shape=[jax.ShapeDtypeStruct((b, GQA_Q_HEADS, s, hd), MXU_DTYPE),
                   jax.ShapeDtypeStruct((b, GQA_KV_HEADS, hd, s), MXU_DTYPE),
                   jax.ShapeDtypeStruct((b, GQA_KV_HEADS, s, hd), MXU_DTYPE)],
        compiler_params=_params("arbitrary", "arbitrary"), name="gqa_qkv",
    )(x, modt, ng, w_qkv.astype(MXU_DTYPE), q_gain.reshape(1, hd), k_gain.reshape(1, hd), cos, sin)


def _diff_qkv_kernel(x_ref, mod_ref, ng_ref, w_ref, cos_ref, sin_ref, q_ref, k_ref, v_ref):
    mod = mod_ref[0]
    d = x_ref.shape[2]
    h = _rms(x_ref[0], ng_ref[0:1]) * (1.0 + mod[1:2]) + mod[0:1]
    qkv = _dot(h, w_ref[...])
    cos, sin = cos_ref[...], sin_ref[...]
    hw = 2 * DIFF_HEAD_DIM
    lane = lax.broadcasted_iota(jnp.int32, (x_ref.shape[1], hw), 1)
    quarter = DIFF_HEAD_DIM // 2
    take_up = (lane // quarter) % 2 == 0

    def rope(t):
        rot = jnp.where(take_up, pltpu.roll(t, hw - quarter, 1), pltpu.roll(t, quarter, 1))
        return t * cos + rot * sin

    scale = DIFF_HEAD_DIM ** -0.5 * LOG2E
    first = lane < DIFF_HEAD_DIM
    for i in range(DIFF_HEADS):
        q = rope(qkv[:, i * hw:(i + 1) * hw]) * scale
        q_ref[0, 2 * i] = jnp.where(first, q, 0.0).astype(q_ref.dtype)
        q_ref[0, 2 * i + 1] = jnp.where(first, 0.0, q).astype(q_ref.dtype)
        k_ref[0, i] = rope(qkv[:, d + i * hw:d + (i + 1) * hw]).T.astype(k_ref.dtype)
        v_ref[0, i] = qkv[:, 2 * d + i * hw:2 * d + (i + 1) * hw].astype(v_ref.dtype)


def _diff_qkv(x, modt, ng, w_qkv, cos, sin):
    b, s, d = x.shape
    hw = 2 * DIFF_HEAD_DIM
    n = w_qkv.shape[1]
    return pl.pallas_call(
        _diff_qkv_kernel, grid=(b, s // TM),
        in_specs=[pl.BlockSpec((1, TM, d), lambda bi, j: (bi, j, 0)),
                  pl.BlockSpec((1, SUBLANES, d), lambda bi, j: (bi * 2 + jnp.minimum(j, 1), 0, 0)),
                  pl.BlockSpec((4, d), lambda bi, j: (0, 0)),
                  pl.BlockSpec((d, n), lambda bi, j: (0, 0)),
                  pl.BlockSpec((TM, hw), lambda bi, j: (j, 0)),
                  pl.BlockSpec((TM, hw), lambda bi, j: (j, 0))],
        out_specs=[pl.BlockSpec((1, 2 * DIFF_HEADS, TM, hw), lambda bi, j: (bi, 0, j, 0)),
                   pl.BlockSpec((1, DIFF_HEADS, hw, TM), lambda bi, j: (bi, 0, 0, j)),
                   pl.BlockSpec((1, DIFF_HEADS, TM, hw), lambda bi, j: (bi, 0, j, 0))],
        out_shape=[jax.ShapeDtypeStruct((b, 2 * DIFF_HEADS, s, hw), MXU_DTYPE),
                   jax.ShapeDtypeStruct((b, DIFF_HEADS, hw, s), MXU_DTYPE),
                   jax.ShapeDtypeStruct((b, DIFF_HEADS, s, hw), MXU_DTYPE)],
        compiler_params=_params("arbitrary", "arbitrary"), name="diff_qkv",
    )(x, modt, ng, w_qkv.astype(MXU_DTYPE), cos, sin)


def _flash_kernel(*refs, group, ctx_len, kv_chunk, n_chunks, lam_init):
    if lam_init is None:
        q_ref, kt_ref, v_ref, o_ref, s_even, s_odd = refs
    else:
        q_ref, kt_ref, v_ref, lam_ref, sg_ref, o_ref, s_even, s_odd = refs
    s_bufs = (s_even, s_odd)
    tq, hd = q_ref.shape[2], q_ref.shape[3]
    j = pl.program_id(2)
    rows = group * tq
    q = q_ref[0].reshape(rows, hd)
    n_sub = kv_chunk // SCORE_COLS

    def qk(start):
        return jnp.dot(q, kt_ref[0, 0, :, pl.ds(start, SCORE_COLS)], preferred_element_type=F32)

    def probs(s, m):
        return jnp.exp2((s - m).astype(v_ref.dtype))

    def pv(p, start):
        v = v_ref[0, 0, pl.ds(start, SCORE_COLS), :]
        return jnp.dot(p, jnp.concatenate([v, jnp.ones_like(v)], axis=1), preferred_element_type=F32)

    def fold(x):
        return [x[:, c * LANES:(c + 1) * LANES] for c in range(SCORE_COLS // LANES)]

    def chunk_start(i):
        start = ctx_len + i * kv_chunk
        return start if isinstance(i, int) else pl.multiple_of(start, SCORE_COLS)

    s = qk(0)
    m = jnp.max(s, axis=-1, keepdims=True)
    acc = pv(probs(s, m), 0)

    def latent_keys(state):
        def scores_and_max(i, buf, c, mx):
            sc = qk(chunk_start(i) + c * SCORE_COLS)
            buf[:, c * SCORE_COLS:(c + 1) * SCORE_COLS] = sc
            for part in fold(sc):
                mx = part if mx is None else jnp.maximum(mx, part)
            return mx

        def new_max(m, mx):
            m_new = jnp.maximum(m, jnp.max(mx, axis=-1, keepdims=True))
            return m_new, jnp.exp2(m - m_new)

        def step(i, parity, has_next, m, alpha, acc):
            acc = alpha * acc
            mx = None
            for c in range(n_sub):
                if has_next:
                    mx = scores_and_max(i + 1, s_bufs[1 - parity], c, mx)
                pc = probs(s_bufs[parity][:, c * SCORE_COLS:(c + 1) * SCORE_COLS], m)
                acc = acc + pv(pc, chunk_start(i) + c * SCORE_COLS)
            if has_next:
                m, alpha = new_max(m, mx)
            return m, alpha, acc

        m, acc = state
        mx = None
        for c in range(n_sub):
            mx = scores_and_max(0, s_bufs[0], c, mx)
        m, alpha = new_max(m, mx)
        pairs = (n_chunks - 1) // 2

        def body(t, carry):
            carry = step(2 * t, 0, True, *carry)
            return step(2 * t + 1, 1, True, *carry)

        carry = lax.fori_loop(0, pairs, body, (m, alpha, acc))
        for i in range(2 * pairs, n_chunks):
            carry = step(i, i % 2, i < n_chunks - 1, *carry)
        return carry[0], carry[2]

    _, acc = lax.cond(j > 0, latent_keys, lambda state: state, (m, acc))
    o = acc[:, :hd] / acc[:, hd:]
    if lam_init is None:
        for g in range(group):
            o_ref[0, :, g * hd:(g + 1) * hd] = o[g * tq:(g + 1) * tq].astype(o_ref.dtype)
    else:
        lv = lam_ref[...]
        lam = (jnp.exp(jnp.sum(lv[0:1] * lv[1:2], axis=-1, keepdims=True))
               - jnp.exp(jnp.sum(lv[2:3] * lv[3:4], axis=-1, keepdims=True)) + lam_init)
        od = o[:tq] - lam * o[tq:]
        o_ref[0] = (_rms(od, sg_ref[...]) * (1.0 - lam_init)).astype(o_ref.dtype)


def _flash(q, k, v, group, ctx_len, lam=None, subln=None, lam_init=None):
    b, hq, s, hd = q.shape
    hkv = k.shape[1]
    lat = s - ctx_len
    kv_chunk = min(KV_CHUNK, lat)
    assert ctx_len == SCORE_COLS and kv_chunk % SCORE_COLS == 0 and lat % kv_chunk == 0
    kern =functools.partial(_flash_kernel, group=group, ctx_len=ctx_len, kv_chunk=kv_chunk,
                             n_chunks=lat // kv_chunk, lam_init=lam_init)
    in_specs = [pl.BlockSpec((1, group, TM, hd), lambda bi, h, j: (bi, h, j, 0)),
                pl.BlockSpec((1, 1, hd, s), lambda bi, h, j: (bi, h, 0, 0)),
                pl.BlockSpec((1, 1, s, hd), lambda bi, h, j: (bi, h, 0, 0))]
    args = [q, k, v]
    if lam_init is None:
        out_w = group * hd
    else:
        out_w = hd
        in_specs += [pl.BlockSpec(lam.shape, lambda bi, h, j: (0, 0)),
                     pl.BlockSpec((1, hd), lambda bi, h, j: (0, 0))]
        args += [lam, subln.reshape(1, hd)]
    return pl.pallas_call(
        kern, grid=(b, hkv, s // TM), in_specs=in_specs,
        out_specs=pl.BlockSpec((1, TM, out_w), lambda bi, h, j: (bi, j, h)),
        out_shape=jax.ShapeDtypeStruct((b, s, hkv * out_w), MXU_DTYPE),
        scratch_shapes=[pltpu.VMEM((group * TM, kv_chunk), F32), pltpu.VMEM((group * TM, kv_chunk), F32)],
        compiler_params=_params("arbitrary", "arbitrary", "arbitrary"), name="flash_attention",
    )(*args)


def _attn_out_body(xt, mod, ng, o_ref, wo_ref):
    return _dot(o_ref[0], wo_ref[...])


def _attn_out_layer(x, modt, ng, router, o, w_o):
    b, s, d = x.shape
    specs = [pl.BlockSpec((1, TM, d), lambda bi, j: (bi, j, 0)),
             pl.BlockSpec((d, d), lambda bi, j: (0, 0))]
    return _mixer_call(_attn_out_body, x, modt, ng, router, (o, w_o.astype(MXU_DTYPE)), specs, "attn_out")


def _rope_tables(ctx_len, lat_len, dim, reps):
    n = dim // 4
    pos = jnp.arange(lat_len, dtype=jnp.int32)
    row = (pos // GRID_W).astype(F32)
    col = (pos % GRID_W).astype(F32)
    freqs = ROPE_THETA ** (-jnp.arange(n, dtype=F32) / n)
    ang = jnp.concatenate([row[:, None] * freqs, col[:, None] * freqs], axis=-1)
    ang = jnp.concatenate([jnp.zeros((ctx_len, dim // 2), F32), ang], axis=0)
    cos, sin = jnp.cos(ang), jnp.sin(ang)
    return jnp.tile(cos, (1, 2 * reps)), jnp.tile(jnp.concatenate([-sin, sin], axis=-1), (1, reps))


N_ZERO_BLOCKS = 2 * N_EXPERTS
DISPATCH_BUFS = 3


def _dispatch_kernel(zero_ref, flag_ref, dest_ref, h_hbm, hs_hbm, zero_buf, tiles, zero_sem, load_sem, row_sem):
    i = pl.program_id(0)
    n = pl.num_programs(0)

    @pl.when(i == 0)
    def _():
        zero_buf[...] = jnp.zeros_like(zero_buf)

        def zero_copy(e):
            start = pl.multiple_of(zero_ref[e], SUBLANES)
            return pltpu.make_async_copy(zero_buf, hs_hbm.at[pl.ds(start, EXPERT_ROWS)], zero_sem)

        for e in range(N_ZERO_BLOCKS):
            pl.when(flag_ref[e] != 0)(lambda e=e: zero_copy(e).start())
        for e in range(N_ZERO_BLOCKS):
            pl.when(flag_ref[e] != 0)(lambda e=e: zero_copy(e).wait())

    def tile_load(t, slot):
        return pltpu.make_async_copy(h_hbm.at[pl.ds(pl.multiple_of(t * TM, TM), TM)], tiles.at[slot],
                                     load_sem.at[slot])

    def rows_done(slot):
        for _ in range(TOP_K):
            pltpu.make_async_copy(tiles.at[slot], tiles.at[slot], row_sem.at[slot]).wait()

    slot = i % DISPATCH_BUFS

    @pl.when(i == 0)
    def _():
        tile_load(0, 0).start()

    @pl.when(i + 1 < n)
    def _():
        tile_load(i + 1, (i + 1) % DISPATCH_BUFS).start()

    tile_load(i, slot).wait()
    for r in range(TM):
        for k in range(TOP_K):
            pltpu.make_async_copy(tiles.at[slot, pl.ds(r, 1)], hs_hbm.at[pl.ds(dest_ref[0, 0, r * TOP_K + k], 1)],
                                  row_sem.at[slot]).start()

    @pl.when(i >= 1)
    def _():
        rows_done((i + DISPATCH_BUFS - 1) % DISPATCH_BUFS)

    @pl.when(i == n - 1)
    def _():
        rows_done(slot)


def _dispatch(h2, dest3, zero_start, zero_flag, n_pad):
    t, d = h2.shape
    grid_spec = pltpu.PrefetchScalarGridSpec(
        num_scalar_prefetch=2, grid=(t // TM,),
        in_specs=[pl.BlockSpec((1, 1, TM * TOP_K), lambda i, zs, zf: (i, 0, 0), memory_space=pltpu.SMEM),
                  pl.BlockSpec(memory_space=pl.ANY)],
        out_specs=pl.BlockSpec(memory_space=pl.ANY),
        scratch_shapes=[pltpu.VMEM((EXPERT_ROWS, d), F32), pltpu.VMEM((DISPATCH_BUFS, TM, d), F32),
                        pltpu.SemaphoreType.DMA, pltpu.SemaphoreType.DMA((DISPATCH_BUFS,)),
                        pltpu.SemaphoreType.DMA((DISPATCH_BUFS,))])
    return pl.pallas_call(
        _dispatch_kernel, grid_spec=grid_spec,
        out_shape=jax.ShapeDtypeStruct((n_pad, d), F32),
        compiler_params=pltpu.CompilerParams(dimension_semantics=("arbitrary",), has_side_effects=True),
        name="moe_dispatch",
    )(zero_start, zero_flag, dest3, h2)


def _expert_kernel(be_ref, nused_ref, hs_ref, w1_ref, b1_ref, w2_ref, b2_ref, o_ref, w1_mxu, w2_mxu):
    i = pl.program_id(0)
    f = w2_ref.shape[1]

    @pl.when((i == 0) | (be_ref[i] != be_ref[jnp.maximum(i - 1, 0)]))
    def _():
        w1_mxu[...] = w1_ref[0].astype(w1_mxu.dtype)
        w2_mxu[...] = w2_ref[0].astype(w2_mxu.dtype)

    @pl.when(i < nused_ref[0])
    def _():
        gu = _dot(hs_ref[...], w1_mxu[...]) + b1_ref[0]
        g = jnp.minimum(gu[:, :f], SWIGLU_LIMIT)
        u = jnp.clip(gu[:, f:], -SWIGLU_LIMIT, SWIGLU_LIMIT)
        act = g * _sigmoid(SWIGLU_ALPHA * g) * (u + 1.0)
        o_ref[...] = _dot(act, w2_mxu[...]) + b2_ref[0]

    @pl.when(i >= nused_ref[0])
    def _():
        o_ref[...] = jnp.zeros_like(o_ref)


def _experts(hs, block_e, n_used, w1, b1, w2, b2, layer):
    n_pad, d = hs.shape
    _, e, _, f2 = w1.shape
    f = w2.shape[2]
    tb = EXPERT_ROWS
    grid_spec = pltpu.PrefetchScalarGridSpec(
        num_scalar_prefetch=2, grid=(n_pad // tb,),
        in_specs=[pl.BlockSpec((tb, d), lambda i, be, nu: (i, 0)),
                  pl.BlockSpec((None, 1, d, f2), lambda i, be, nu: (layer, be[i], 0, 0)),
                  pl.BlockSpec((1, 1, f2), lambda i, be, nu: (be[i], 0, 0)),
                  pl.BlockSpec((None, 1, f, d), lambda i, be, nu: (layer, be[i], 0, 0)),
                  pl.BlockSpec((1, 1, d), lambda i, be, nu: (be[i], 0, 0))],
        out_specs=pl.BlockSpec((tb, d), lambda i, be, nu: (i, 0)),
        scratch_shapes=[pltpu.VMEM((d, f2), MXU_DTYPE), pltpu.VMEM((f, d), MXU_DTYPE)])
    return pl.pallas_call(
        _expert_kernel, grid_spec=grid_spec,
        out_shape=jax.ShapeDtypeStruct((n_pad, d), F32),
        compiler_params=_params("arbitrary"), name="moe_experts",
    )(block_e, n_used, hs, w1, b1.reshape(e, 1, f2), w2, b2.reshape(e, 1, d))


def _combine_kernel(dest_ref, dest_next_ref, ys_hbm, mf_ref, x1_ref, mod_ref, ng_ref, x2_ref, buf, sem):
    tile = pl.program_id(0) * pl.num_programs(1) + pl.program_id(1)
    n_tiles = pl.num_programs(0) * pl.num_programs(1)
    slot = tile % 2

    def fetch(idx_ref, to_slot):
        for r in range(TM):
            for k in range(TOP_K):
                pltpu.make_async_copy(ys_hbm.at[pl.ds(idx_ref[0, 0, r * TOP_K + k], 1)],
                                      buf.at[to_slot, k, pl.ds(r, 1)], sem.at[to_slot]).start()

    @pl.when(tile == 0)
    def _():
        fetch(dest_ref, 0)

    @pl.when(tile + 1 < n_tiles)
    def _():
        fetch(dest_next_ref, 1 - slot)

    pltpu.make_async_copy(buf.at[slot], buf.at[slot], sem.at[slot]).wait()
    gates = mf_ref[...]
    y = gates[:, 0:1] * buf[slot, 0]
    for k in range(1, TOP_K):
        y = y + gates[:, k:k + 1] * buf[slot, k]
    mod = mod_ref[0]
    x2_ref[0] = x1_ref[0] + mod[5:6] * _rms(y, ng_ref[3:4])


def _combine(ys, dest3, mf, x1, modt, ng, latent_only):
    b, s, d = x1.shape
    nt = s // TM
    if latent_only:
        out_spec = pl.BlockSpec((1, TM, d), lambda bi, j: (bi, jnp.maximum(j - 1, 0), 0))
        out_shape = jax.ShapeDtypeStruct((b, s - TM, d), F32)
    else:
        out_spec = pl.BlockSpec((1, TM, d), lambda bi, j: (bi, j, 0))
        out_shape = jax.ShapeDtypeStruct((b, s, d), F32)
    return pl.pallas_call(
        _combine_kernel, grid=(b, nt),
        in_specs=[pl.BlockSpec((1, 1, TM * TOP_K), lambda bi, j: (bi * nt + j, 0, 0), memory_space=pltpu.SMEM),
                  pl.BlockSpec((1, 1, TM * TOP_K), lambda bi, j: (jnp.minimum(bi * nt + j + 1, b * nt - 1), 0, 0),
                               memory_space=pltpu.SMEM),
                  pl.BlockSpec(memory_space=pl.ANY),
                  pl.BlockSpec((TM, LANES), lambda bi, j: (bi * nt + j, 0)),
                  pl.BlockSpec((1, TM, d), lambda bi, j: (bi, j, 0)),
                  pl.BlockSpec((1, SUBLANES, d), lambda bi, j: (bi * 2 + jnp.minimum(j, 1), 0, 0)),
                  pl.BlockSpec((4, d), lambda bi, j: (0, 0))],
        out_specs=out_spec, out_shape=out_shape,
        scratch_shapes=[pltpu.VMEM((2, TOP_K, TM, d), F32), pltpu.SemaphoreType.DMA((2,))],
        compiler_params=_params("arbitrary", "arbitrary"), name="moe_combine",
    )(dest3, dest3, ys, mf, x1, modt, ng)


def _moe(x1, h2, mi, mf, cnt, modt, ng, w1, b1, w2, b2, layer, latent_only):
    t = h2.shape[0]
    tb = EXPERT_ROWS
    counts = cnt[0, :N_EXPERTS].astype(jnp.int32)
    padded = (counts + tb - 1) // tb * tb
    pend = jnp.cumsum(padded)
    pstart = pend - padded
    dest = pstart[mi[:, :TOP_K]] + mi[:, TOP_K:2 * TOP_K]
    n_blocks = -(-(t * TOP_K) // tb) + N_EXPERTS
    block_row0 = jnp.arange(n_blocks, dtype=jnp.int32) * tb
    block_e = jnp.minimum(jnp.sum((pend[None, :] <= block_row0[:, None]).astype(jnp.int32), axis=1),
                          N_EXPERTS - 1)
    n_used = (pend[-1:] // tb).astype(jnp.int32)
    dest3 = dest.reshape(t // TM, 1, TM * TOP_K)
    tail_start = jnp.maximum(pend - tb, 0).astype(jnp.int32)
    trailing = block_row0[n_blocks - N_EXPERTS:]
    zero_start = jnp.concatenate([tail_start, trailing])
    zero_flag = jnp.concatenate([padded > 0, trailing >= pend[-1]]).astype(jnp.int32)
    hs = _dispatch(h2, dest3, zero_start, zero_flag, n_blocks * tb)
    ys = _experts(hs, block_e, n_used, w1, b1, w2, b2, layer)
    return _combine(ys, dest3, mf, x1, modt, ng, latent_only)


def kernel(x, c, ctx, c_ctx, mod_w, mod_b, norm_g, pool_w, pool_scale, gqa_w_qkv, gqa_w_o, gqa_q_gain, gqa_k_gain, conv_w_in, conv_w_dw, conv_w_out, diff_w_qkv, diff_w_o, diff_lambda, diff_subln_gain, router_w, router_b, moe_w1, moe_b1, moe_w2, moe_b2):
    b, lat_len, d = x.shape
    ctx_len = ctx.shape[1]
    depth = mod_w.shape[0]
    n_mixers = 4
    assert ctx_len == TM and lat_len % TM == 0 and b + 1 <= SUBLANES and d % LANES == 0

    xa = jnp.concatenate([ctx, x], axis=1)

    cv = jnp.zeros((SUBLANES, d), F32).at[:b].set(c).at[b].set(c_ctx)
    mods = _modulation(cv, mod_w, mod_b).reshape(depth, SUBLANES, MOD_CHUNKS, d)
    modt_all = jnp.stack([jnp.broadcast_to(mods[:, b][:, None], (depth, b, MOD_CHUNKS, d)), mods[:, :b]], axis=2)
    modt_all = jnp.pad(modt_all, ((0, 0), (0, 0), (0, 0), (0, SUBLANES - MOD_CHUNKS), (0, 0)))
    modt_all = modt_all.reshape(depth, b * 2, SUBLANES, d)

    cos_b, sin_b = _rope_tables(ctx_len, lat_len, GQA_HEAD_DIM, 1)
    cos_d, sin_d = _rope_tables(ctx_len, lat_len, DIFF_HEAD_DIM, 2)

    for i in range(depth):
        m, jj = i % n_mixers, i // n_mixers
        modt, ng = modt_all[i], norm_g[i]
        rw = jnp.zeros((d, LANES), F32).at[:, :N_EXPERTS].set(router_w[i])
        rwh = rw.astype(jnp.bfloat16)
        rwl = (rw - rwh.astype(F32)).astype(jnp.bfloat16)
        rb = jnp.full((1, LANES), NEG_BIG, F32).at[0, :N_EXPERTS].set(router_b[i])
        router = (rwh, rwl, rb)
        if m == 0:
            res = _pool_layer(xa, modt, ng, router, pool_w[jj], pool_scale[jj], ctx_len)
        elif m == 1:
            q, k, v = _gqa_qkv(xa, modt, ng, gqa_w_qkv[jj], gqa_q_gain[jj], gqa_k_gain[jj], cos_b, sin_b)
            o = _flash(q, k, v, GQA_Q_HEADS // GQA_KV_HEADS, ctx_len)
            res = _attn_out_layer(xa, modt, ng, router, o, gqa_w_o[jj])
        elif m == 2:
            res = _conv_layer(xa, modt, ng, router, conv_w_in[jj], conv_w_dw[jj], conv_w_out[jj])
        else:
            lam_init = 0.8 - 0.6 * math.exp(-0.3 * i)
            q, k, v = _diff_qkv(xa, modt, ng, diff_w_qkv[jj], cos_d, sin_d)
            o = _flash(q, k, v, 2, ctx_len, lam=diff_lambda[jj], subln=diff_subln_gain[jj], lam_init=lam_init)
            res = _attn_out_layer(xa, modt, ng, router, o, diff_w_o[jj])
        x1, h2, mi, mf, cnt = res
        xa = _moe(x1, h2, mi, mf, cnt, modt, ng,
                  moe_w1, moe_b1[i], moe_w2, moe_b2[i], layer=i, latent_only=(i == depth - 1))
    return xa
```

```python
import functools
import math

import jax
import jax.numpy as jnp
from jax import lax
from jax.experimental import pallas as pl
from jax.experimental.pallas import tpu as pltpu

RMS_EPS = 1e-6
ROPE_THETA = 10000.0
GRID_W = 64
POOL_WINDOWS = (2, 4, 8, 16)
N_EXPERTS = 32
TOP_K = 4
SWIGLU_LIMIT = 7.0
SWIGLU_ALPHA = 1.702
GQA_HEAD_DIM = 128
GQA_Q_HEADS = 8
GQA_KV_HEADS = 2
DIFF_HEAD_DIM = 64
DIFF_HEADS = 8
MOD_CHUNKS = 6

LANES = 128
SUBLANES = 8
VMEM_LIMIT_BYTES = 56 * 1024 * 1024

TM = 256
EXPERT_ROWS = 512
KV_CHUNK = 2048
SCORE_COLS = 256
LOG2E = math.log2(math.e)
HALO = SUBLANES

MXU_DTYPE = jnp.bfloat16
F32 = jnp.float32
NEG_BIG = -1e30


def _params(*sem):
    return pltpu.CompilerParams(dimension_semantics=sem, vmem_limit_bytes=VMEM_LIMIT_BYTES)


def _dot(a, b):
    return jnp.dot(a.astype(MXU_DTYPE), b.astype(MXU_DTYPE), preferred_element_type=F32)


def _rms(x, g):
    return x * lax.rsqrt(jnp.mean(x * x, axis=-1, keepdims=True) + RMS_EPS) * g


def _sigmoid(z):
    return 1.0 / (1.0 + jnp.exp(-z))


def _shift_up(a, s):
    n = a.shape[0]
    return pltpu.roll(a, (n - s) % n, 0)


def _mod_kernel(cv_ref, w_ref, b_ref, o_ref):
    a = cv_ref[...]
    o_ref[0] = _dot(a * _sigmoid(a), w_ref[0]) + b_ref[0]


def _modulation(cv, mod_w, mod_b):
    depth, d, n = mod_w.shape
    tn = n // 4
    return pl.pallas_call(
        _mod_kernel,
        grid=(depth, n // tn),
        in_specs=[pl.BlockSpec((SUBLANES, d), lambda i, k: (0, 0)),
                  pl.BlockSpec((1, d, tn), lambda i, k: (i, 0, k)),
                  pl.BlockSpec((1, 1, tn), lambda i, k: (i, 0, k))],
        out_specs=pl.BlockSpec((1, SUBLANES, tn), lambda i, k: (i, 0, k)),
        out_shape=jax.ShapeDtypeStruct((depth, SUBLANES, n), F32),
        compiler_params=_params("arbitrary", "arbitrary"),
        name="modulation",
    )(cv, mod_w, mod_b.reshape(depth, 1, n))


def _post_mixer(y, x, mod, ng, rwh_ref, rwl_ref, rb_ref, base_ref,
                x1_ref, h2_ref, mi_ref, mf_ref, cnt_ref):
    tm = x.shape[0]
    x1 = x + mod[2:3] * _rms(y, ng[1:2])
    h2 = _rms(x1, ng[2:3]) * (1.0 + mod[4:5]) + mod[3:4]
    x1_ref[0] = x1
    h2_ref[...] = h2

    h_hi = h2.astype(jnp.bfloat16)
    h_lo = (h2 - h_hi.astype(F32)).astype(jnp.bfloat16)
    rwh = rwh_ref[...]
    logits = (jnp.dot(h_hi, rwh, preferred_element_type=F32)
              + jnp.dot(h_lo, rwh, preferred_element_type=F32)
              + jnp.dot(h_hi, rwl_ref[...], preferred_element_type=F32)) + rb_ref[...]

    lane = lax.broadcasted_iota(jnp.int32, (tm, LANES), 1)
    lane_f = lane.astype(F32)
    vals, idxs, hots = [], [], []
    l = logits
    for _ in range(TOP_K):
        m = jnp.max(l, axis=-1, keepdims=True)
        idx = jnp.min(jnp.where(l == m, lane_f, float(LANES)), axis=-1, keepdims=True)
        hot = lane_f == idx
        vals.append(m)
        idxs.append(idx)
        hots.append(hot)
        l = jnp.where(hot, -3e38, l)
    exps = [jnp.exp(v - vals[0]) for v in vals]
    den = exps[0] + exps[1] + exps[2] + exps[3]

    @pl.when((pl.program_id(0) == 0) & (pl.program_id(1) == 0))
    def _():
        base_ref[...] = jnp.zeros_like(base_ref)

    hot_all = (hots[0].astype(F32) + hots[1].astype(F32) + hots[2].astype(F32) + hots[3].astype(F32))
    rows = lax.broadcasted_iota(jnp.int32, (tm, tm), 0)
    cols = lax.broadcasted_iota(jnp.int32, (tm, tm), 1)
    lower = (rows > cols).astype(jnp.bfloat16)
    before = jnp.dot(lower, hot_all.astype(jnp.bfloat16), preferred_element_type=F32) + base_ref[...]
    mi = jnp.zeros((tm, LANES), F32)
    mf = jnp.zeros((tm, LANES), F32)
    for k in range(TOP_K):
        rank = jnp.sum(jnp.where(hots[k], before, 0.0), axis=-1, keepdims=True)
        mi = mi + jnp.where(lane == k, idxs[k], 0.0) + jnp.where(lane == TOP_K + k, rank, 0.0)
        mf = mf + jnp.where(lane == k, exps[k] / den, 0.0)
    mi_ref[...] = mi.astype(jnp.int32)
    mf_ref[...] = mf
    base_ref[...] = base_ref[...] + jnp.sum(hot_all, axis=0, keepdims=True)
    cnt_ref[...] = base_ref[...]


def _mixer_call(body, x, modt, ng, router, extra_args, extra_specs, name):
    b, s, d = x.shape
    nt = s // TM
    rwh, rwl, rb = router
    n_extra = len(extra_args)

    def kern(*refs):
        x_ref, mod_ref, ng_ref, rwh_ref, rwl_ref, rb_ref = refs[:6]
        extra = refs[6:6 + n_extra]
        x1_ref, h2_ref, mi_ref, mf_ref, cnt_ref, base_ref = refs[6 + n_extra:]
        xt = x_ref[0]
        mod = mod_ref[0]
        ngv = ng_ref[...]
        y = body(xt, mod, ngv, *extra)
        _post_mixer(y, xt, mod, ngv, rwh_ref, rwl_ref, rb_ref, base_ref,
                    x1_ref, h2_ref, mi_ref, mf_ref, cnt_ref)

    in_specs = [
        pl.BlockSpec((1, TM, d), lambda bi, j: (bi, j, 0)),
        pl.BlockSpec((1, SUBLANES, d), lambda bi, j: (bi * 2 + jnp.minimum(j, 1), 0, 0)),
        pl.BlockSpec((4, d), lambda bi, j: (0, 0)),
        pl.BlockSpec((d, LANES), lambda bi, j: (0, 0)),
        pl.BlockSpec((d, LANES), lambda bi, j: (0, 0)),
        pl.BlockSpec((1, LANES), lambda bi, j: (0, 0)),
    ] + list(extra_specs)
    out_specs = [
        pl.BlockSpec((1, TM, d), lambda bi, j: (bi, j, 0)),
        pl.BlockSpec((TM, d), lambda bi, j: (bi * nt + j, 0)),
        pl.BlockSpec((TM, LANES), lambda bi, j: (bi * nt + j, 0)),
        pl.BlockSpec((TM, LANES), lambda bi, j: (bi * nt + j, 0)),
        pl.BlockSpec((1, LANES), lambda bi, j: (0, 0)),
    ]
    out_shape = [
        jax.ShapeDtypeStruct((b, s, d), F32),
        jax.ShapeDtypeStruct((b * s, d), F32),
        jax.ShapeDtypeStruct((b * s, LANES), jnp.int32),
        jax.ShapeDtypeStruct((b * s, LANES), F32),
        jax.ShapeDtypeStruct((1, LANES), F32),
    ]
    return pl.pallas_call(
        kern, grid=(b, nt), in_specs=in_specs, out_specs=out_specs, out_shape=out_shape,
        scratch_shapes=[pltpu.VMEM((1, LANES), F32)],
        compiler_params=_params("arbitrary", "arbitrary"), name=name,
    )(x, modt, ng, rwh, rwl, rb, *extra_args)


def _halo_specs(s, d):
    last = s // HALO - 1
    per_tile = TM // HALO
    prev = pl.BlockSpec((1, HALO, d), lambda bi, j: (bi, jnp.maximum(j * per_tile - 1, 0), 0))
    nxt = pl.BlockSpec((1, HALO, d), lambda bi, j: (bi, jnp.minimum((j + 1) * per_tile, last), 0))
    return prev, nxt


def _halo_valid():
    j = pl.program_id(1)
    nt = pl.num_programs(1)
    return j >= 2, (j >= 1) & (j <= nt - 2)


def _pool_body(ctx_len, lat_len, xt, mod, ng, xp_ref, xn_ref, pw_ref, ps_ref):
    tm, d = xt.shape
    group = d // len(POOL_WINDOWS)
    prev_ok, next_ok = _halo_valid()
    pre = lambda rows: _rms(rows, ng[0:1]) * (1.0 + mod[1:2]) + mod[0:1]
    hc = pre(xt)
    hp = jnp.where(prev_ok, pre(xp_ref[0]), 0.0)
    hn = jnp.where(next_ok, pre(xn_ref[0]), 0.0)
    ext = jnp.concatenate([hp, hc, hn], axis=0)

    j = pl.program_id(1)
    pos = lax.broadcasted_iota(jnp.int32, (tm, 1), 0) + jnp.where(j == 0, 0, (j - 1) * tm)
    seq_len = jnp.where(j == 0, ctx_len, lat_len)
    outs = []
    for g, w in enumerate(POOL_WINDOWS):
        e = ext[:, g * group:(g + 1) * group]
        run, span = e, 1
        while span < w:
            run = run + _shift_up(run, span)
            span *= 2
        win = _shift_up(run, HALO - w // 2)[:tm] if w // 2 != HALO else run[:tm]
        lo = jnp.maximum(pos - w // 2, 0)
        hi = jnp.minimum(pos + (w - w // 2), seq_len)
        diff = win / (hi - lo).astype(F32) - hc[:, g * group:(g + 1) * group]
        outs.append(_dot(diff, pw_ref[g]))
    return jnp.concatenate(outs, axis=1) * ps_ref[...]


def _pool_layer(x, modt, ng, router, pool_w, pool_scale, ctx_len):
    b, s, d = x.shape
    prev, nxt = _halo_specs(s, d)
    g = len(POOL_WINDOWS)
    specs = [prev, nxt,
             pl.BlockSpec((g, d // g, d // g), lambda bi, j: (0, 0, 0)),
             pl.BlockSpec((1, d), lambda bi, j: (0, 0))]
    body = functools.partial(_pool_body, ctx_len, s - ctx_len)
    return _mixer_call(body, x, modt, ng, router,
                       (x, x, pool_w.astype(MXU_DTYPE), pool_scale.reshape(1, d)), specs, "pool_mixer")


def _conv_body(xt, mod, ng, xp_ref, xn_ref, win_ref, wdw_ref, wout_ref):
    tm, d = xt.shape
    prev_ok, next_ok = _halo_valid()
    rows = jnp.concatenate([xp_ref[0], xt, xn_ref[0]], axis=0)
    h = _rms(rows, ng[0:1]) * (1.0 + mod[1:2]) + mod[0:1]
    proj = _dot(h, win_ref[...])
    gate_b = proj[HALO:HALO + tm, :d]
    u = proj[:, d:2 * d] * proj[:, 2 * d:]
    r = lax.broadcasted_iota(jnp.int32, (tm + 2 * HALO, 1), 0)
    keep = ((r >= HALO) | prev_ok) & ((r < HALO + tm) | next_ok)
    u = jnp.where(keep, u, 0.0)
    wdw = wdw_ref[...]
    conv = (wdw[0:1] * _shift_up(u, HALO - 1)[:tm] + wdw[1:2] * u[HALO:HALO + tm]
            + wdw[2:3] * _shift_up(u, HALO + 1)[:tm])
    return _dot(gate_b * conv, wout_ref[...])


def _conv_layer(x, modt, ng, router, w_in, w_dw, w_out):
    b, s, d = x.shape
    prev, nxt = _halo_specs(s, d)
    specs = [prev, nxt,
             pl.BlockSpec((d, 3 * d), lambda bi, j: (0, 0)),
             pl.BlockSpec((SUBLANES, d), lambda bi, j: (0, 0)),
             pl.BlockSpec((d, d), lambda bi, j: (0, 0))]
    w_dw8 = jnp.zeros((SUBLANES, d), F32).at[:w_dw.shape[0]].set(w_dw)
    return _mixer_call(_conv_body, x, modt, ng, router,
                       (x, x, w_in.astype(MXU_DTYPE), w_dw8, w_out.astype(MXU_DTYPE)), specs, "conv_mixer")


def _gqa_qkv_kernel(x_ref, mod_ref, ng_ref, w_ref, qg_ref, kg_ref, cos_ref, sin_ref,
                    q_ref, k_ref, v_ref):
    mod = mod_ref[0]
    h = _rms(x_ref[0], ng_ref[0:1]) * (1.0 + mod[1:2]) + mod[0:1]
    qkv = _dot(h, w_ref[...])
    cos, sin = cos_ref[...], sin_ref[...]
    hd = GQA_HEAD_DIM

    def rope(t):
        return t * cos + pltpu.roll(t, hd // 2, 1) * sin

    scale = hd ** -0.5 * LOG2E
    for i in range(GQA_Q_HEADS):
        q = rope(_rms(qkv[:, i * hd:(i + 1) * hd], qg_ref[...]))
        q_ref[0, i] = (q * scale).astype(q_ref.dtype)
    for i in range(GQA_KV_HEADS):
        o = (GQA_Q_HEADS + i) * hd
        k_ref[0, i] = rope(_rms(qkv[:, o:o + hd], kg_ref[...])).T.astype(k_ref.dtype)
        o = (GQA_Q_HEADS + GQA_KV_HEADS + i) * hd
        v_ref[0, i] = qkv[:, o:o + hd].astype(v_ref.dtype)


def _gqa_qkv(x, modt, ng, w_qkv, q_gain, k_gain, cos, sin):
    b, s, d = x.shape
    hd = GQA_HEAD_DIM
    n = w_qkv.shape[1]
    return pl.pallas_call(
        _gqa_qkv_kernel, grid=(b, s // TM),
        in_specs=[pl.BlockSpec((1, TM, d), lambda bi, j: (bi, j, 0)),
                  pl.BlockSpec((1, SUBLANES, d), lambda bi, j: (bi * 2 + jnp.minimum(j, 1), 0, 0)),
                  pl.BlockSpec((4, d), lambda bi, j: (0, 0)),
                  pl.BlockSpec((d, n), lambda bi, j: (0, 0)),
                  pl.BlockSpec((1, hd), lambda bi, j: (0, 0)),
                  pl.BlockSpec((1, hd), lambda bi, j: (0, 0)),
                  pl.BlockSpec((TM, hd), lambda bi, j: (j, 0)),
                  pl.BlockSpec((TM, hd), lambda bi, j: (j, 0))],
        out_specs=[pl.BlockSpec((1, GQA_Q_HEADS, TM, hd), lambda bi, j: (bi, 0, j, 0)),
                   pl.BlockSpec((1, GQA_KV_HEADS, hd, TM), lambda bi, j: (bi, 0, 0, j)),
                   pl.BlockSpec((1, GQA_KV_HEADS, TM, hd), lambda bi, j: (bi, 0, j, 0))],
        out_shape=[jax.ShapeDtypeStruct((b, GQA_Q_HEADS, s, hd), MXU_DTYPE),
                   jax.ShapeDtypeStruct((b, GQA_KV_HEADS, hd, s), MXU_DTYPE),
                   jax.ShapeDtypeStruct((b, GQA_KV_HEADS, s, hd), MXU_DTYPE)],
        compiler_params=_params("arbitrary", "arbitrary"), name="gqa_qkv",
    )(x, modt, ng, w_qkv.astype(MXU_DTYPE), q_gain.reshape(1, hd), k_gain.reshape(1, hd), cos, sin)


def _diff_qkv_kernel(x_ref, mod_ref, ng_ref, w_ref, cos_ref, sin_ref, q_ref, k_ref, v_ref):
    mod = mod_ref[0]
    d = x_ref.shape[2]
    h = _rms(x_ref[0], ng_ref[0:1]) * (1.0 + mod[1:2]) + mod[0:1]
    qkv = _dot(h, w_ref[...])
    cos, sin = cos_ref[...], sin_ref[...]
    hw = 2 * DIFF_HEAD_DIM
    lane = lax.broadcasted_iota(jnp.int32, (x_ref.shape[1], hw), 1)
    quarter = DIFF_HEAD_DIM // 2
    take_up = (lane // quarter) % 2 == 0

    def rope(t):
        rot = jnp.where(take_up, pltpu.roll(t, hw - quarter, 1), pltpu.roll(t, quarter, 1))
        return t * cos + rot * sin

    scale = DIFF_HEAD_DIM ** -0.5 * LOG2E
    first = lane < DIFF_HEAD_DIM
    for i in range(DIFF_HEADS):
        q = rope(qkv[:, i * hw:(i + 1) * hw]) * scale
        q_ref[0, 2 * i] = jnp.where(first, q, 0.0).astype(q_ref.dtype)
        q_ref[0, 2 * i + 1] = jnp.where(first, 0.0, q).astype(q_ref.dtype)
        k_ref[0, i] = rope(qkv[:, d + i * hw:d + (i + 1) * hw]).T.astype(k_ref.dtype)
        v_ref[0, i] = qkv[:, 2 * d + i * hw:2 * d + (i + 1) * hw].astype(v_ref.dtype)


def _diff_qkv(x, modt, ng, w_qkv, cos, sin):
    b, s, d = x.shape
    hw = 2 * DIFF_HEAD_DIM
    n = w_qkv.shape[1]
    return pl.pallas_call(
        _diff_qkv_kernel, grid=(b, s // TM),
        in_specs=[pl.BlockSpec((1, TM, d), lambda bi, j: (bi, j, 0)),
                  pl.BlockSpec((1, SUBLANES, d), lambda bi, j: (bi * 2 + jnp.minimum(j, 1), 0, 0)),
                  pl.BlockSpec((4, d), lambda bi, j: (0, 0)),
                  pl.BlockSpec((d, n), lambda bi, j: (0, 0)),
                  pl.BlockSpec((TM, hw), lambda bi, j: (j, 0)),
                  pl.BlockSpec((TM, hw), lambda bi, j: (j, 0))],
        out_specs=[pl.BlockSpec((1, 2 * DIFF_HEADS, TM, hw), lambda bi, j: (bi, 0, j, 0)),
                   pl.BlockSpec((1, DIFF_HEADS, hw, TM), lambda bi, j: (bi, 0, 0, j)),
                   pl.BlockSpec((1, DIFF_HEADS, TM, hw), lambda bi, j: (bi, 0, j, 0))],
        out_shape=[jax.ShapeDtypeStruct((b, 2 * DIFF_HEADS, s, hw), MXU_DTYPE),
                   jax.ShapeDtypeStruct((b, DIFF_HEADS, hw, s), MXU_DTYPE),
                   jax.ShapeDtypeStruct((b, DIFF_HEADS, s, hw), MXU_DTYPE)],
        compiler_params=_params("arbitrary", "arbitrary"), name="diff_qkv",
    )(x, modt, ng, w_qkv.astype(MXU_DTYPE), cos, sin)


def _flash_kernel(*refs, group, ctx_len, kv_chunk, n_chunks, lam_init):
    if lam_init is None:
        q_ref, kt_ref, v_ref, o_ref, s_even, s_odd = refs
    else:
        q_ref, kt_ref, v_ref, lam_ref, sg_ref, o_ref, s_even, s_odd = refs
    s_bufs = (s_even, s_odd)
    tq, hd = q_ref.shape[2], q_ref.shape[3]
    j = pl.program_id(2)
    rows = group * tq
    q = q_ref[0].reshape(rows, hd)
    n_sub = kv_chunk // SCORE_COLS

    def qk(start):
        return jnp.dot(q, kt_ref[0, 0, :, pl.ds(start, SCORE_COLS)], preferred_element_type=F32)

    def probs(s, m):
        return jnp.exp2((s - m).astype(v_ref.dtype))

    def pv(p, start):
        v = v_ref[0, 0, pl.ds(start, SCORE_COLS), :]
        return jnp.dot(p, jnp.concatenate([v, jnp.ones_like(v)], axis=1), preferred_element_type=F32)

    def fold(x):
        return [x[:, c * LANES:(c + 1) * LANES] for c in range(SCORE_COLS // LANES)]

    def chunk_start(i):
        start = ctx_len + i * kv_chunk
        return start if isinstance(i, int) else pl.multiple_of(start, SCORE_COLS)

    s = qk(0)
    m = jnp.max(s, axis=-1, keepdims=True)
    acc = pv(probs(s, m), 0)

    def latent_keys(state):
        def scores_and_max(i, buf, c, mx):
            sc = qk(chunk_start(i) + c * SCORE_COLS)
            buf[:, c * SCORE_COLS:(c + 1) * SCORE_COLS] = sc
            for part in fold(sc):
                mx = part if mx is None else jnp.maximum(mx, part)
            return mx

        def new_max(m, mx):
            m_new = jnp.maximum(m, jnp.max(mx, axis=-1, keepdims=True))
            return m_new, jnp.exp2(m - m_new)

        def step(i, parity, has_next, m, alpha, acc):
            acc = alpha * acc
            mx = None
            for c in range(n_sub):
                if has_next:
                    mx = scores_and_max(i + 1, s_bufs[1 - parity], c, mx)
                pc = probs(s_bufs[parity][:, c * SCORE_COLS:(c + 1) * SCORE_COLS], m)
                acc = acc + pv(pc, chunk_start(i) + c * SCORE_COLS)
            if has_next:
                m, alpha = new_max(m, mx)
            return m, alpha, acc

        m, acc = state
        mx = None
        for c in range(n_sub):
            mx = scores_and_max(0, s_bufs[0], c, mx)
        m, alpha = new_max(m, mx)
        pairs = (n_chunks - 1) // 2

        def body(t, carry):
            carry = step(2 * t, 0, True, *carry)
            return step(2 * t + 1, 1, True, *carry)

        carry = lax.fori_loop(0, pairs, body, (m, alpha, acc))
        for i in range(2 * pairs, n_chunks):
            carry = step(i, i % 2, i < n_chunks - 1, *carry)
        return carry[0], carry[2]

    _, acc = lax.cond(j > 0, latent_keys, lambda state: state, (m, acc))
    o = acc[:, :hd] / acc[:, hd:]
    if lam_init is None:
        for g in range(group):
            o_ref[0, :, g * hd:(g + 1) * hd] = o[g * tq:(g + 1) * tq].astype(o_ref.dtype)
    else:
        lv = lam_ref[...]
        lam = (jnp.exp(jnp.sum(lv[0:1] * lv[1:2], axis=-1, keepdims=True))
               - jnp.exp(jnp.sum(lv[2:3] * lv[3:4], axis=-1, keepdims=True)) + lam_init)
        od = o[:tq] - lam * o[tq:]
        o_ref[0] = (_rms(od, sg_ref[...]) * (1.0 - lam_init)).astype(o_ref.dtype)


def _flash(q, k, v, group, ctx_len, lam=None, subln=None, lam_init=None):
    b, hq, s, hd = q.shape
    hkv = k.shape[1]
    lat = s - ctx_len
    kv_chunk = min(KV_CHUNK, lat)
    assert ctx_len == SCORE_COLS and kv_chunk % SCORE_COLS == 0 and lat % kv_chunk == 0
    kern =functools.partial(_flash_kernel, group=group, ctx_len=ctx_len, kv_chunk=kv_chunk,
                             n_chunks=lat // kv_chunk, lam_init=lam_init)
    in_specs = [pl.BlockSpec((1, group, TM, hd), lambda bi, h, j: (bi, h, j, 0)),
                pl.BlockSpec((1, 1, hd, s), lambda bi, h, j: (bi, h, 0, 0)),
                pl.BlockSpec((1, 1, s, hd), lambda bi, h, j: (bi, h, 0, 0))]
    args = [q, k, v]
    if lam_init is None:
        out_w = group * hd
    else:
        out_w = hd
        in_specs += [pl.BlockSpec(lam.shape, lambda bi, h, j: (0, 0)),
                     pl.BlockSpec((1, hd), lambda bi, h, j: (0, 0))]
        args += [lam, subln.reshape(1, hd)]
    return pl.pallas_call(
        kern, grid=(b, hkv, s // TM), in_specs=in_specs,
        out_specs=pl.BlockSpec((1, TM, out_w), lambda bi, h, j: (bi, j, h)),
        out_shape=jax.ShapeDtypeStruct((b, s, hkv * out_w), MXU_DTYPE),
        scratch_shapes=[pltpu.VMEM((group * TM, kv_chunk), F32), pltpu.VMEM((group * TM, kv_chunk), F32)],
        compiler_params=_params("arbitrary", "arbitrary", "arbitrary"), name="flash_attention",
    )(*args)


def _attn_out_body(xt, mod, ng, o_ref, wo_ref):
    return _dot(o_ref[0], wo_ref[...])


def _attn_out_layer(x, modt, ng, router, o, w_o):
    b, s, d = x.shape
    specs = [pl.BlockSpec((1, TM, d), lambda bi, j: (bi, j, 0)),
             pl.BlockSpec((d, d), lambda bi, j: (0, 0))]
    return _mixer_call(_attn_out_body, x, modt, ng, router, (o, w_o.astype(MXU_DTYPE)), specs, "attn_out")


def _rope_tables(ctx_len, lat_len, dim, reps):
    n = dim // 4
    pos = jnp.arange(lat_len, dtype=jnp.int32)
    row = (pos // GRID_W).astype(F32)
    col = (pos % GRID_W).astype(F32)
    freqs = ROPE_THETA ** (-jnp.arange(n, dtype=F32) / n)
    ang = jnp.concatenate([row[:, None] * freqs, col[:, None] * freqs], axis=-1)
    ang = jnp.concatenate([jnp.zeros((ctx_len, dim // 2), F32), ang], axis=0)
    cos, sin = jnp.cos(ang), jnp.sin(ang)
    return jnp.tile(cos, (1, 2 * reps)), jnp.tile(jnp.concatenate([-sin, sin], axis=-1), (1, reps))


N_ZERO_BLOCKS = 2 * N_EXPERTS
DISPATCH_BUFS = 3


def _dispatch_kernel(zero_ref, flag_ref, dest_ref, h_hbm, hs_hbm, zero_buf, tiles, zero_sem, load_sem, row_sem):
    i = pl.program_id(0)
    n = pl.num_programs(0)

    @pl.when(i == 0)
    def _():
        zero_buf[...] = jnp.zeros_like(zero_buf)

        def zero_copy(e):
            start = pl.multiple_of(zero_ref[e], SUBLANES)
            return pltpu.make_async_copy(zero_buf, hs_hbm.at[pl.ds(start, EXPERT_ROWS)], zero_sem)

        for e in range(N_ZERO_BLOCKS):
            pl.when(flag_ref[e] != 0)(lambda e=e: zero_copy(e).start())
        for e in range(N_ZERO_BLOCKS):
            pl.when(flag_ref[e] != 0)(lambda e=e: zero_copy(e).wait())

    def tile_load(t, slot):
        return pltpu.make_async_copy(h_hbm.at[pl.ds(pl.multiple_of(t * TM, TM), TM)], tiles.at[slot],
                                     load_sem.at[slot])

    def rows_done(slot):
        for _ in range(TOP_K):
            pltpu.make_async_copy(tiles.at[slot], tiles.at[slot], row_sem.at[slot]).wait()

    slot = i % DISPATCH_BUFS

    @pl.when(i == 0)
    def _():
        tile_load(0, 0).start()

    @pl.when(i + 1 < n)
    def _():
        tile_load(i + 1, (i + 1) % DISPATCH_BUFS).start()

    tile_load(i, slot).wait()
    for r in range(TM):
        for k in range(TOP_K):
            pltpu.make_async_copy(tiles.at[slot, pl.ds(r, 1)], hs_hbm.at[pl.ds(dest_ref[0, 0, r * TOP_K + k], 1)],
                                  row_sem.at[slot]).start(priority=k % 2)

    @pl.when(i >= 1)
    def _():
        rows_done((i + DISPATCH_BUFS - 1) % DISPATCH_BUFS)

    @pl.when(i == n - 1)
    def _():
        rows_done(slot)


def _dispatch(h2, dest3, zero_start, zero_flag, n_pad):
    t, d = h2.shape
    grid_spec = pltpu.PrefetchScalarGridSpec(
        num_scalar_prefetch=2, grid=(t // TM,),
        in_specs=[pl.BlockSpec((1, 1, TM * TOP_K), lambda i, zs, zf: (i, 0, 0), memory_space=pltpu.SMEM),
                  pl.BlockSpec(memory_space=pl.ANY)],
        out_specs=pl.BlockSpec(memory_space=pl.ANY),
        scratch_shapes=[pltpu.VMEM((EXPERT_ROWS, d), F32), pltpu.VMEM((DISPATCH_BUFS, TM, d), F32),
                        pltpu.SemaphoreType.DMA, pltpu.SemaphoreType.DMA((DISPATCH_BUFS,)),
                        pltpu.SemaphoreType.DMA((DISPATCH_BUFS,))])
    return pl.pallas_call(
        _dispatch_kernel, grid_spec=grid_spec,
        out_shape=jax.ShapeDtypeStruct((n_pad, d), F32),
        compiler_params=pltpu.CompilerParams(dimension_semantics=("arbitrary",), has_side_effects=True),
        name="moe_dispatch",
    )(zero_start, zero_flag, dest3, h2)


def _expert_kernel(be_ref, nused_ref, hs_ref, w1_ref, b1_ref, w2_ref, b2_ref, o_ref, w1_mxu, w2_mxu):
    i = pl.program_id(0)
    f = w2_ref.shape[1]

    @pl.when((i == 0) | (be_ref[i] != be_ref[jnp.maximum(i - 1, 0)]))
    def _():
        w1_mxu[...] = w1_ref[0].astype(w1_mxu.dtype)
        w2_mxu[...] = w2_ref[0].astype(w2_mxu.dtype)

    @pl.when(i < nused_ref[0])
    def _():
        gu = _dot(hs_ref[...], w1_mxu[...]) + b1_ref[0]
        g = jnp.minimum(gu[:, :f], SWIGLU_LIMIT)
        u = jnp.clip(gu[:, f:], -SWIGLU_LIMIT, SWIGLU_LIMIT)
        act = g * _sigmoid(SWIGLU_ALPHA * g) * (u + 1.0)
        o_ref[...] = _dot(act, w2_mxu[...]) + b2_ref[0]

    @pl.when(i >= nused_ref[0])
    def _():
        o_ref[...] = jnp.zeros_like(o_ref)


def _experts(hs, block_e, n_used, w1, b1, w2, b2, layer):
    n_pad, d = hs.shape
    _, e, _, f2 = w1.shape
    f = w2.shape[2]
    tb = EXPERT_ROWS
    grid_spec = pltpu.PrefetchScalarGridSpec(
        num_scalar_prefetch=2, grid=(n_pad // tb,),
        in_specs=[pl.BlockSpec((tb, d), lambda i, be, nu: (i, 0)),
                  pl.BlockSpec((None, 1, d, f2), lambda i, be, nu: (layer, be[i], 0, 0)),
                  pl.BlockSpec((1, 1, f2), lambda i, be, nu: (be[i], 0, 0)),
                  pl.BlockSpec((None, 1, f, d), lambda i, be, nu: (layer, be[i], 0, 0)),
                  pl.BlockSpec((1, 1, d), lambda i, be, nu: (be[i], 0, 0))],
        out_specs=pl.BlockSpec((tb, d), lambda i, be, nu: (i, 0)),
        scratch_shapes=[pltpu.VMEM((d, f2), MXU_DTYPE), pltpu.VMEM((f, d), MXU_DTYPE)])
    return pl.pallas_call(
        _expert_kernel, grid_spec=grid_spec,
        out_shape=jax.ShapeDtypeStruct((n_pad, d), F32),
        compiler_params=_params("arbitrary"), name="moe_experts",
    )(block_e, n_used, hs, w1, b1.reshape(e, 1, f2), w2, b2.reshape(e, 1, d))


def _combine_kernel(dest_ref, dest_next_ref, ys_hbm, mf_ref, x1_ref, mod_ref, ng_ref, x2_ref, buf, sem):
    tile = pl.program_id(0) * pl.num_programs(1) + pl.program_id(1)
    n_tiles = pl.num_programs(0) * pl.num_programs(1)
    slot = tile % 2

    def fetch(idx_ref, to_slot):
        for r in range(TM):
            for k in range(TOP_K):
                pltpu.make_async_copy(ys_hbm.at[pl.ds(idx_ref[0, 0, r * TOP_K + k], 1)],
                                      buf.at[to_slot, k, pl.ds(r, 1)], sem.at[to_slot]).start(priority=k % 2)

    @pl.when(tile == 0)
    def _():
        fetch(dest_ref, 0)

    @pl.when(tile + 1 < n_tiles)
    def _():
        fetch(dest_next_ref, 1 - slot)

    pltpu.make_async_copy(buf.at[slot], buf.at[slot], sem.at[slot]).wait()
    gates = mf_ref[...]
    y = gates[:, 0:1] * buf[slot, 0]
    for k in range(1, TOP_K):
        y = y + gates[:, k:k + 1] * buf[slot, k]
    mod = mod_ref[0]
    x2_ref[0] = x1_ref[0] + mod[5:6] * _rms(y, ng_ref[3:4])


def _combine(ys, dest3, mf, x1, modt, ng, latent_only):
    b, s, d = x1.shape
    nt = s // TM
    if latent_only:
        out_spec = pl.BlockSpec((1, TM, d), lambda bi, j: (bi, jnp.maximum(j - 1, 0), 0))
        out_shape = jax.ShapeDtypeStruct((b, s - TM, d), F32)
    else:
        out_spec = pl.BlockSpec((1, TM, d), lambda bi, j: (bi, j, 0))
        out_shape = jax.ShapeDtypeStruct((b, s, d), F32)
    return pl.pallas_call(
        _combine_kernel, grid=(b, nt),
        in_specs=[pl.BlockSpec((1, 1, TM * TOP_K), lambda bi, j: (bi * nt + j, 0, 0), memory_space=pltpu.SMEM),
                  pl.BlockSpec((1, 1, TM * TOP_K), lambda bi, j: (jnp.minimum(bi * nt + j + 1, b * nt - 1), 0, 0),
                               memory_space=pltpu.SMEM),
                  pl.BlockSpec(memory_space=pl.ANY),
                  pl.BlockSpec((TM, LANES), lambda bi, j: (bi * nt + j, 0)),
                  pl.BlockSpec((1, TM, d), lambda bi, j: (bi, j, 0)),
                  pl.BlockSpec((1, SUBLANES, d), lambda bi, j: (bi * 2 + jnp.minimum(j, 1), 0, 0)),
                  pl.BlockSpec((4, d), lambda bi, j: (0, 0))],
        out_specs=out_spec, out_shape=out_shape,
        scratch_shapes=[pltpu.VMEM((2, TOP_K, TM, d), F32), pltpu.SemaphoreType.DMA((2,))],
        compiler_params=_params("arbitrary", "arbitrary"), name="moe_combine",
    )(dest3, dest3, ys, mf, x1, modt, ng)


def _moe(x1, h2, mi, mf, cnt, modt, ng, w1, b1, w2, b2, layer, latent_only):
    t = h2.shape[0]
    tb = EXPERT_ROWS
    counts = cnt[0, :N_EXPERTS].astype(jnp.int32)
    padded = (counts + tb - 1) // tb * tb
    pend = jnp.cumsum(padded)
    pstart = pend - padded
    dest = pstart[mi[:, :TOP_K]] + mi[:, TOP_K:2 * TOP_K]
    n_blocks = -(-(t * TOP_K) // tb) + N_EXPERTS
    block_row0 = jnp.arange(n_blocks, dtype=jnp.int32) * tb
    block_e = jnp.minimum(jnp.sum((pend[None, :] <= block_row0[:, None]).astype(jnp.int32), axis=1),
                          N_EXPERTS - 1)
    n_used = (pend[-1:] // tb).astype(jnp.int32)
    dest3 = dest.reshape(t // TM, 1, TM * TOP_K)
    tail_start = jnp.maximum(pend - tb, 0).astype(jnp.int32)
    trailing = block_row0[n_blocks - N_EXPERTS:]
    zero_start = jnp.concatenate([tail_start, trailing])
    zero_flag = jnp.concatenate([padded > 0, trailing >= pend[-1]]).astype(jnp.int32)
    hs = _dispatch(h2, dest3, zero_start, zero_flag, n_blocks * tb)
    ys = _experts(hs, block_e, n_used, w1, b1, w2, b2, layer)
    return _combine(ys, dest3, mf, x1, modt, ng, latent_only)


def kernel(x, c, ctx, c_ctx, mod_w, mod_b, norm_g, pool_w, pool_scale, gqa_w_qkv, gqa_w_o, gqa_q_gain, gqa_k_gain, conv_w_in, conv_w_dw, conv_w_out, diff_w_qkv, diff_w_o, diff_lambda, diff_subln_gain, router_w, router_b, moe_w1, moe_b1, moe_w2, moe_b2):
    b, lat_len, d = x.shape
    ctx_len = ctx.shape[1]
    depth = mod_w.shape[0]
    n_mixers = 4
    assert ctx_len == TM and lat_len % TM == 0 and b + 1 <= SUBLANES and d % LANES == 0

    xa = jnp.concatenate([ctx, x], axis=1)

    cv = jnp.zeros((SUBLANES, d), F32).at[:b].set(c).at[b].set(c_ctx)
    mods = _modulation(cv, mod_w, mod_b).reshape(depth, SUBLANES, MOD_CHUNKS, d)
    modt_all = jnp.stack([jnp.broadcast_to(mods[:, b][:, None], (depth, b, MOD_CHUNKS, d)), mods[:, :b]], axis=2)
    modt_all = jnp.pad(modt_all, ((0, 0), (0, 0), (0, 0), (0, SUBLANES - MOD_CHUNKS), (0, 0)))
    modt_all = modt_all.reshape(depth, b * 2, SUBLANES, d)

    cos_b, sin_b = _rope_tables(ctx_len, lat_len, GQA_HEAD_DIM, 1)
    cos_d, sin_d = _rope_tables(ctx_len, lat_len, DIFF_HEAD_DIM, 2)

    for i in range(depth):
        m, jj = i % n_mixers, i // n_mixers
        modt, ng = modt_all[i], norm_g[i]
        rw = jnp.zeros((d, LANES), F32).at[:, :N_EXPERTS].set(router_w[i])
        rwh = rw.astype(jnp.bfloat16)
        rwl = (rw - rwh.astype(F32)).astype(jnp.bfloat16)
        rb = jnp.full((1, LANES), NEG_BIG, F32).at[0, :N_EXPERTS].set(router_b[i])
        router = (rwh, rwl, rb)
        if m == 0:
            res = _pool_layer(xa, modt, ng, router, pool_w[jj], pool_scale[jj], ctx_len)
        elif m == 1:
            q, k, v = _gqa_qkv(xa, modt, ng, gqa_w_qkv[jj], gqa_q_gain[jj], gqa_k_gain[jj], cos_b, sin_b)
            o = _flash(q, k, v, GQA_Q_HEADS // GQA_KV_HEADS, ctx_len)
            res = _attn_out_layer(xa, modt, ng, router, o, gqa_w_o[jj])
        elif m == 2:
            res = _conv_layer(xa, modt, ng, router, conv_w_in[jj], conv_w_dw[jj], conv_w_out[jj])
        else:
            lam_init = 0.8 - 0.6 * math.exp(-0.3 * i)
            q, k, v = _diff_qkv(xa, modt, ng, diff_w_qkv[jj], cos_d, sin_d)
            o = _flash(q, k, v, 2, ctx_len, lam=diff_lambda[jj], subln=diff_subln_gain[jj], lam_init=lam_init)
            res = _attn_out_layer(xa, modt, ng, router, o, diff_w_o[jj])
        x1, h2, mi, mf, cnt = res
        xa = _moe(x1, h2, mi, mf, cnt, modt, ng,
                  moe_w1, moe_b1[i], moe_w2, moe_b2[i], layer=i, latent_only=(i == depth - 1))
    return xa
```

```python
import functools
import math

import jax
import jax.numpy as jnp
from jax import lax
from jax.experimental import pallas as pl
from jax.experimental.pallas import tpu as pltpu

RMS_EPS = 1e-6
ROPE_THETA = 10000.0
GRID_W = 64
POOL_WINDOWS = (2, 4, 8, 16)
N_EXPERTS = 32
TOP_K = 4
SWIGLU_LIMIT = 7.0
SWIGLU_ALPHA = 1.702
GQA_HEAD_DIM = 128
GQA_Q_HEADS = 8
GQA_KV_HEADS = 2
DIFF_HEAD_DIM = 64
DIFF_HEADS = 8
MOD_CHUNKS = 6

LANES = 128
SUBLANES = 8
VMEM_LIMIT_BYTES = 56 * 1024 * 1024

TM = 256
EXPERT_ROWS = 512
KV_CHUNK = 2048
SCORE_COLS = 256
GQA_STACK = 2
LOG2E = math.log2(math.e)
HALO = SUBLANES

MXU_DTYPE = jnp.bfloat16
F32 = jnp.float32
NEG_BIG = -1e30


def _params(*sem):
    return pltpu.CompilerParams(dimension_semantics=sem, vmem_limit_bytes=VMEM_LIMIT_BYTES)


def _dot(a, b):
    return jnp.dot(a.astype(MXU_DTYPE), b.astype(MXU_DTYPE), preferred_element_type=F32)


def _rms(x, g):
    return x * lax.rsqrt(jnp.mean(x * x, axis=-1, keepdims=True) + RMS_EPS) * g


def _sigmoid(z):
    return 1.0 / (1.0 + jnp.exp(-z))


def _shift_up(a, s):
    n = a.shape[0]
    return pltpu.roll(a, (n - s) % n, 0)


def _mod_kernel(cv_ref, w_ref, b_ref, o_ref):
    a = cv_ref[...]
    o_ref[0] = _dot(a * _sigmoid(a), w_ref[0]) + b_ref[0]


def _modulation(cv, mod_w, mod_b):
    depth, d, n = mod_w.shape
    tn = n // 4
    return pl.pallas_call(
        _mod_kernel,
        grid=(depth, n // tn),
        in_specs=[pl.BlockSpec((SUBLANES, d), lambda i, k: (0, 0)),
                  pl.BlockSpec((1, d, tn), lambda i, k: (i, 0, k)),
                  pl.BlockSpec((1, 1, tn), lambda i, k: (i, 0, k))],
        out_specs=pl.BlockSpec((1, SUBLANES, tn), lambda i, k: (i, 0, k)),
        out_shape=jax.ShapeDtypeStruct((depth, SUBLANES, n), F32),
        compiler_params=_params("arbitrary", "arbitrary"),
        name="modulation",
    )(cv, mod_w, mod_b.reshape(depth, 1, n))


def _post_mixer(y, x, mod, ng, rwh_ref, rwl_ref, rb_ref, base_ref,
                x1_ref, h2_ref, mi_ref, mf_ref, cnt_ref):
    tm = x.shape[0]
    x1 = x + mod[2:3] * _rms(y, ng[1:2])
    h2 = _rms(x1, ng[2:3]) * (1.0 + mod[4:5]) + mod[3:4]
    x1_ref[0] = x1
    h2_ref[...] = h2

    h_hi = h2.astype(jnp.bfloat16)
    h_lo = (h2 - h_hi.astype(F32)).astype(jnp.bfloat16)
    rwh = rwh_ref[...]
    logits = (jnp.dot(h_hi, rwh, preferred_element_type=F32)
              + jnp.dot(h_lo, rwh, preferred_element_type=F32)
              + jnp.dot(h_hi, rwl_ref[...], preferred_element_type=F32)) + rb_ref[...]

    lane = lax.broadcasted_iota(jnp.int32, (tm, LANES), 1)
    lane_f = lane.astype(F32)
    vals, idxs, hots = [], [], []
    l = logits
    for _ in range(TOP_K):
        m = jnp.max(l, axis=-1, keepdims=True)
        idx = jnp.min(jnp.where(l == m, lane_f, float(LANES)), axis=-1, keepdims=True)
        hot = lane_f == idx
        vals.append(m)
        idxs.append(idx)
        hots.append(hot)
        l = jnp.where(hot, -3e38, l)
    exps = [jnp.exp(v - vals[0]) for v in vals]
    den = exps[0] + exps[1] + exps[2] + exps[3]

    @pl.when((pl.program_id(0) == 0) & (pl.program_id(1) == 0))
    def _():
        base_ref[...] = jnp.zeros_like(base_ref)

    hot_all = (hots[0].astype(F32) + hots[1].astype(F32) + hots[2].astype(F32) + hots[3].astype(F32))
    rows = lax.broadcasted_iota(jnp.int32, (tm, tm), 0)
    cols = lax.broadcasted_iota(jnp.int32, (tm, tm), 1)
    lower = (rows > cols).astype(jnp.bfloat16)
    before = jnp.dot(lower, hot_all.astype(jnp.bfloat16), preferred_element_type=F32) + base_ref[...]
    mi = jnp.zeros((tm, LANES), F32)
    mf = jnp.zeros((tm, LANES), F32)
    for k in range(TOP_K):
        rank = jnp.sum(jnp.where(hots[k], before, 0.0), axis=-1, keepdims=True)
        mi = mi + jnp.where(lane == k, idxs[k], 0.0) + jnp.where(lane == TOP_K + k, rank, 0.0)
        mf = mf + jnp.where(lane == k, exps[k] / den, 0.0)
    mi_ref[...] = mi.astype(jnp.int32)
    mf_ref[...] = mf
    base_ref[...] = base_ref[...] + jnp.sum(hot_all, axis=0, keepdims=True)
    cnt_ref[...] = base_ref[...]


def _mixer_call(body, x, modt, ng, router, extra_args, extra_specs, name):
    b, s, d = x.shape
    nt = s // TM
    rwh, rwl, rb = router
    n_extra = len(extra_args)

    def kern(*refs):
        x_ref, mod_ref, ng_ref, rwh_ref, rwl_ref, rb_ref = refs[:6]
        extra = refs[6:6 + n_extra]
        x1_ref, h2_ref, mi_ref, mf_ref, cnt_ref, base_ref = refs[6 + n_extra:]
        xt = x_ref[0]
        mod = mod_ref[0]
        ngv = ng_ref[...]
        y = body(xt, mod, ngv, *extra)
        _post_mixer(y, xt, mod, ngv, rwh_ref, rwl_ref, rb_ref, base_ref,
                    x1_ref, h2_ref, mi_ref, mf_ref, cnt_ref)

    in_specs = [
        pl.BlockSpec((1, TM, d), lambda bi, j: (bi, j, 0)),
        pl.BlockSpec((1, SUBLANES, d), lambda bi, j: (bi * 2 + jnp.minimum(j, 1), 0, 0)),
        pl.BlockSpec((4, d), lambda bi, j: (0, 0)),
        pl.BlockSpec((d, LANES), lambda bi, j: (0, 0)),
        pl.BlockSpec((d, LANES), lambda bi, j: (0, 0)),
        pl.BlockSpec((1, LANES), lambda bi, j: (0, 0)),
    ] + list(extra_specs)
    out_specs = [
        pl.BlockSpec((1, TM, d), lambda bi, j: (bi, j, 0)),
        pl.BlockSpec((TM, d), lambda bi, j: (bi * nt + j, 0)),
        pl.BlockSpec((TM, LANES), lambda bi, j: (bi * nt + j, 0)),
        pl.BlockSpec((TM, LANES), lambda bi, j: (bi * nt + j, 0)),
        pl.BlockSpec((1, LANES), lambda bi, j: (0, 0)),
    ]
    out_shape = [
        jax.ShapeDtypeStruct((b, s, d), F32),
        jax.ShapeDtypeStruct((b * s, d), F32),
        jax.ShapeDtypeStruct((b * s, LANES), jnp.int32),
        jax.ShapeDtypeStruct((b * s, LANES), F32),
        jax.ShapeDtypeStruct((1, LANES), F32),
    ]
    return pl.pallas_call(
        kern, grid=(b, nt), in_specs=in_specs, out_specs=out_specs, out_shape=out_shape,
        scratch_shapes=[pltpu.VMEM((1, LANES), F32)],
        compiler_params=_params("arbitrary", "arbitrary"), name=name,
    )(x, modt, ng, rwh, rwl, rb, *extra_args)


def _halo_specs(s, d):
    last = s // HALO - 1
    per_tile = TM // HALO
    prev = pl.BlockSpec((1, HALO, d), lambda bi, j: (bi, jnp.maximum(j * per_tile - 1, 0), 0))
    nxt = pl.BlockSpec((1, HALO, d), lambda bi, j: (bi, jnp.minimum((j + 1) * per_tile, last), 0))
    return prev, nxt


def _halo_valid():
    j = pl.program_id(1)
    nt = pl.num_programs(1)
    return j >= 2, (j >= 1) & (j <= nt - 2)


def _pool_body(ctx_len, lat_len, xt, mod, ng, xp_ref, xn_ref, pw_ref, ps_ref):
    tm, d = xt.shape
    group = d // len(POOL_WINDOWS)
    prev_ok, next_ok = _halo_valid()
    pre = lambda rows: _rms(rows, ng[0:1]) * (1.0 + mod[1:2]) + mod[0:1]
    hc = pre(xt)
    hp = jnp.where(prev_ok, pre(xp_ref[0]), 0.0)
    hn = jnp.where(next_ok, pre(xn_ref[0]), 0.0)
    ext = jnp.concatenate([hp, hc, hn], axis=0)

    j = pl.program_id(1)
    pos = lax.broadcasted_iota(jnp.int32, (tm, 1), 0) + jnp.where(j == 0, 0, (j - 1) * tm)
    seq_len = jnp.where(j == 0, ctx_len, lat_len)
    outs = []
    for g, w in enumerate(POOL_WINDOWS):
        e = ext[:, g * group:(g + 1) * group]
        run, span = e, 1
        while span < w:
            run = run + _shift_up(run, span)
            span *= 2
        win = _shift_up(run, HALO - w // 2)[:tm] if w // 2 != HALO else run[:tm]
        lo = jnp.maximum(pos - w // 2, 0)
        hi = jnp.minimum(pos + (w - w // 2), seq_len)
        diff = win / (hi - lo).astype(F32) - hc[:, g * group:(g + 1) * group]
        outs.append(_dot(diff, pw_ref[g]))
    return jnp.concatenate(outs, axis=1) * ps_ref[...]


def _pool_layer(x, modt, ng, router, pool_w, pool_scale, ctx_len):
    b, s, d = x.shape
    prev, nxt = _halo_specs(s, d)
    g = len(POOL_WINDOWS)
    specs = [prev, nxt,
             pl.BlockSpec((g, d // g, d // g), lambda bi, j: (0, 0, 0)),
             pl.BlockSpec((1, d), lambda bi, j: (0, 0))]
    body = functools.partial(_pool_body, ctx_len, s - ctx_len)
    return _mixer_call(body, x, modt, ng, router,
                       (x, x, pool_w.astype(MXU_DTYPE), pool_scale.reshape(1, d)), specs, "pool_mixer")


def _conv_body(xt, mod, ng, xp_ref, xn_ref, win_ref, wdw_ref, wout_ref):
    tm, d = xt.shape
    prev_ok, next_ok = _halo_valid()
    rows = jnp.concatenate([xp_ref[0], xt, xn_ref[0]], axis=0)
    h = _rms(rows, ng[0:1]) * (1.0 + mod[1:2]) + mod[0:1]
    proj = _dot(h, win_ref[...])
    gate_b = proj[HALO:HALO + tm, :d]
    u = proj[:, d:2 * d] * proj[:, 2 * d:]
    r = lax.broadcasted_iota(jnp.int32, (tm + 2 * HALO, 1), 0)
    keep = ((r >= HALO) | prev_ok) & ((r < HALO + tm) | next_ok)
    u = jnp.where(keep, u, 0.0)
    wdw = wdw_ref[...]
    conv = (wdw[0:1] * _shift_up(u, HALO - 1)[:tm] + wdw[1:2] * u[HALO:HALO + tm]
            + wdw[2:3] * _shift_up(u, HALO + 1)[:tm])
    return _dot(gate_b * conv, wout_ref[...])


def _conv_layer(x, modt, ng, router, w_in, w_dw, w_out):
    b, s, d = x.shape
    prev, nxt = _halo_specs(s, d)
    specs = [prev, nxt,
             pl.BlockSpec((d, 3 * d), lambda bi, j: (0, 0)),
             pl.BlockSpec((SUBLANES, d), lambda bi, j: (0, 0)),
             pl.BlockSpec((d, d), lambda bi, j: (0, 0))]
    w_dw8 = jnp.zeros((SUBLANES, d), F32).at[:w_dw.shape[0]].set(w_dw)
    return _mixer_call(_conv_body, x, modt, ng, router,
                       (x, x, w_in.astype(MXU_DTYPE), w_dw8, w_out.astype(MXU_DTYPE)), specs, "conv_mixer")


def _gqa_qkv_kernel(x_ref, mod_ref, ng_ref, w_ref, qg_ref, kg_ref, cos_ref, sin_ref,
                    q_ref, k_ref, v_ref):
    mod = mod_ref[0]
    h = _rms(x_ref[0], ng_ref[0:1]) * (1.0 + mod[1:2]) + mod[0:1]
    qkv = _dot(h, w_ref[...])
    cos, sin = cos_ref[...], sin_ref[...]
    hd = GQA_HEAD_DIM

    def rope(t):
        return t * cos + pltpu.roll(t, hd // 2, 1) * sin

    scale = hd ** -0.5 * LOG2E
    for i in range(GQA_Q_HEADS):
        q = rope(_rms(qkv[:, i * hd:(i + 1) * hd], qg_ref[...]))
        q_ref[0, i] = (q * scale).astype(q_ref.dtype)
    for i in range(GQA_KV_HEADS):
        o = (GQA_Q_HEADS + i) * hd
        k_ref[0, i] = rope(_rms(qkv[:, o:o + hd], kg_ref[...])).T.astype(k_ref.dtype)
        o = (GQA_Q_HEADS + GQA_KV_HEADS + i) * hd
        v_ref[0, i] = qkv[:, o:o + hd].astype(v_ref.dtype)


def _gqa_qkv(x, modt, ng, w_qkv, q_gain, k_gain, cos, sin):
    b, s, d = x.shape
    hd = GQA_HEAD_DIM
    n = w_qkv.shape[1]
    return pl.pallas_call(
        _gqa_qkv_kernel, grid=(b, s // TM),
        in_specs=[pl.BlockSpec((1, TM, d), lambda bi, j: (bi, j, 0)),
                  pl.BlockSpec((1, SUBLANES, d), lambda bi, j: (bi * 2 + jnp.minimum(j, 1), 0, 0)),
                  pl.BlockSpec((4, d), lambda bi, j: (0, 0)),
                  pl.BlockSpec((d, n), lambda bi, j: (0, 0)),
                  pl.BlockSpec((1, hd), lambda bi, j: (0, 0)),
                  pl.BlockSpec((1, hd), lambda bi, j: (0, 0)),
                  pl.BlockSpec((TM, hd), lambda bi, j: (j, 0)),
                  pl.BlockSpec((TM, hd), lambda bi, j: (j, 0))],
        out_specs=[pl.BlockSpec((1, GQA_Q_HEADS, TM, hd), lambda bi, j: (bi, 0, j, 0)),
                   pl.BlockSpec((1, GQA_KV_HEADS, hd, TM), lambda bi, j: (bi, 0, 0, j)),
                   pl.BlockSpec((1, GQA_KV_HEADS, TM, hd), lambda bi, j: (bi, 0, j, 0))],
        out_shape=[jax.ShapeDtypeStruct((b, GQA_Q_HEADS, s, hd), MXU_DTYPE),
                   jax.ShapeDtypeStruct((b, GQA_KV_HEADS, hd, s), MXU_DTYPE),
                   jax.ShapeDtypeStruct((b, GQA_KV_HEADS, s, hd), MXU_DTYPE)],
        compiler_params=_params("arbitrary", "arbitrary"), name="gqa_qkv",
    )(x, modt, ng, w_qkv.astype(MXU_DTYPE), q_gain.reshape(1, hd), k_gain.reshape(1, hd), cos, sin)


def _diff_qkv_kernel(x_ref, mod_ref, ng_ref, w_ref, cos_ref, sin_ref, q_ref, k_ref, v_ref):
    mod = mod_ref[0]
    d = x_ref.shape[2]
    h = _rms(x_ref[0], ng_ref[0:1]) * (1.0 + mod[1:2]) + mod[0:1]
    qkv = _dot(h, w_ref[...])
    cos, sin = cos_ref[...], sin_ref[...]
    hw = 2 * DIFF_HEAD_DIM
    lane = lax.broadcasted_iota(jnp.int32, (x_ref.shape[1], hw), 1)
    quarter = DIFF_HEAD_DIM // 2
    take_up = (lane // quarter) % 2 == 0

    def rope(t):
        rot = jnp.where(take_up, pltpu.roll(t, hw - quarter, 1), pltpu.roll(t, quarter, 1))
        return t * cos + rot * sin

    scale = DIFF_HEAD_DIM ** -0.5 * LOG2E
    first = lane < DIFF_HEAD_DIM
    for i in range(DIFF_HEADS):
        q = rope(qkv[:, i * hw:(i + 1) * hw]) * scale
        q_ref[0, 2 * i] = jnp.where(first, q, 0.0).astype(q_ref.dtype)
        q_ref[0, 2 * i + 1] = jnp.where(first, 0.0, q).astype(q_ref.dtype)
        k_ref[0, i] = rope(qkv[:, d + i * hw:d + (i + 1) * hw]).T.astype(k_ref.dtype)
        v_ref[0, i] = qkv[:, 2 * d + i * hw:2 * d + (i + 1) * hw].astype(v_ref.dtype)


def _diff_qkv(x, modt, ng, w_qkv, cos, sin):
    b, s, d = x.shape
    hw = 2 * DIFF_HEAD_DIM
    n = w_qkv.shape[1]
    return pl.pallas_call(
        _diff_qkv_kernel, grid=(b, s // TM),
        in_specs=[pl.BlockSpec((1, TM, d), lambda bi, j: (bi, j, 0)),
                  pl.BlockSpec((1, SUBLANES, d), lambda bi, j: (bi * 2 + jnp.minimum(j, 1), 0, 0)),
                  pl.BlockSpec((4, d), lambda bi, j: (0, 0)),
                  pl.BlockSpec((d, n), lambda bi, j: (0, 0)),
                  pl.BlockSpec((TM, hw), lambda bi, j: (j, 0)),
                  pl.BlockSpec((TM, hw), lambda bi, j: (j, 0))],
        out_specs=[pl.BlockSpec((1, 2 * DIFF_HEADS, TM, hw), lambda bi, j: (bi, 0, j, 0)),
                   pl.BlockSpec((1, DIFF_HEADS, hw, TM), lambda bi, j: (bi, 0, 0, j)),
                   pl.BlockSpec((1, DIFF_HEADS, TM, hw), lambda bi, j: (bi, 0, j, 0))],
        out_shape=[jax.ShapeDtypeStruct((b, 2 * DIFF_HEADS, s, hw), MXU_DTYPE),
                   jax.ShapeDtypeStruct((b, DIFF_HEADS, hw, s), MXU_DTYPE),
                   jax.ShapeDtypeStruct((b, DIFF_HEADS, s, hw), MXU_DTYPE)],
        compiler_params=_params("arbitrary", "arbitrary"), name="diff_qkv",
    )(x, modt, ng, w_qkv.astype(MXU_DTYPE), cos, sin)


def _flash_kernel(*refs, group, ctx_len, kv_chunk, n_chunks, lam_init):
    if lam_init is None:
        q_ref, kt_ref, v_ref, o_ref, s_even, s_odd = refs
    else:
        q_ref, kt_ref, v_ref, lam_ref, sg_ref, o_ref, s_even, s_odd = refs
    s_bufs = (s_even, s_odd)
    tq, hd = q_ref.shape[2], q_ref.shape[3]
    j = pl.program_id(2)
    rows = group * tq
    q = q_ref[0].reshape(rows, hd)
    n_sub = kv_chunk // SCORE_COLS

    def qk(start):
        return jnp.dot(q, kt_ref[0, 0, :, pl.ds(start, SCORE_COLS)], preferred_element_type=F32)

    def probs(s, m):
        return jnp.exp2((s - m).astype(v_ref.dtype))

    def pv(p, start):
        v = v_ref[0, 0, pl.ds(start, SCORE_COLS), :]
        return jnp.dot(p, jnp.concatenate([v, jnp.ones_like(v)], axis=1), preferred_element_type=F32)

    def fold(x):
        return [x[:, c * LANES:(c + 1) * LANES] for c in range(SCORE_COLS // LANES)]

    def chunk_start(i):
        start = ctx_len + i * kv_chunk
        return start if isinstance(i, int) else pl.multiple_of(start, SCORE_COLS)

    s = qk(0)
    m = jnp.max(s, axis=-1, keepdims=True)
    acc = pv(probs(s, m), 0)

    def latent_keys(state):
        def scores_and_max(i, buf, c, mx):
            sc = qk(chunk_start(i) + c * SCORE_COLS)
            buf[:, c * SCORE_COLS:(c + 1) * SCORE_COLS] = sc
            for part in fold(sc):
                mx = part if mx is None else jnp.maximum(mx, part)
            return mx

        def new_max(m, mx):
            m_new = jnp.maximum(m, jnp.max(mx, axis=-1, keepdims=True))
            return m_new, jnp.exp2(m - m_new)

        def step(i, parity, has_next, m, alpha, acc):
            acc = alpha * acc
            mx = None
            for c in range(n_sub):
                if has_next:
                    mx = scores_and_max(i + 1, s_bufs[1 - parity], c, mx)
                pc = probs(s_bufs[parity][:, c * SCORE_COLS:(c + 1) * SCORE_COLS], m)
                acc = acc + pv(pc, chunk_start(i) + c * SCORE_COLS)
            if has_next:
                m, alpha = new_max(m, mx)
            return m, alpha, acc

        m, acc = state
        mx = None
        for c in range(n_sub):
            mx = scores_and_max(0, s_bufs[0], c, mx)
        m, alpha = new_max(m, mx)
        pairs = (n_chunks - 1) // 2

        def body(t, carry):
            carry = step(2 * t, 0, True, *carry)
            return step(2 * t + 1, 1, True, *carry)

        carry = lax.fori_loop(0, pairs, body, (m, alpha, acc))
        for i in range(2 * pairs, n_chunks):
            carry = step(i, i % 2, i < n_chunks - 1, *carry)
        return carry[0], carry[2]

    _, acc = lax.cond(j > 0, latent_keys, lambda state: state, (m, acc))
    o = acc[:, :hd] / acc[:, hd:]
    if lam_init is None:
        for g in range(group):
            o_ref[0, :, g * hd:(g + 1) * hd] = o[g * tq:(g + 1) * tq].astype(o_ref.dtype)
    else:
        lv = lam_ref[...]
        lam = (jnp.exp(jnp.sum(lv[0:1] * lv[1:2], axis=-1, keepdims=True))
               - jnp.exp(jnp.sum(lv[2:3] * lv[3:4], axis=-1, keepdims=True)) + lam_init)
        od = o[:tq] - lam * o[tq:]
        o_ref[0] = (_rms(od, sg_ref[...]) * (1.0 - lam_init)).astype(o_ref.dtype)


def _flash(q, k, v, group, ctx_len, lam=None, subln=None, lam_init=None):
    b, hq, s, hd = q.shape
    hkv = hq // group
    share = hkv // k.shape[1]
    lat = s - ctx_len
    kv_chunk = min(KV_CHUNK, lat)
    assert ctx_len == SCORE_COLS and kv_chunk % SCORE_COLS == 0 and lat % kv_chunk == 0
    kern =functools.partial(_flash_kernel, group=group, ctx_len=ctx_len, kv_chunk=kv_chunk,
                             n_chunks=lat // kv_chunk, lam_init=lam_init)
    in_specs = [pl.BlockSpec((1, group, TM, hd), lambda bi, h, j: (bi, h, j, 0)),
                pl.BlockSpec((1, 1, hd, s), lambda bi, h, j: (bi, h // share, 0, 0)),
                pl.BlockSpec((1, 1, s, hd), lambda bi, h, j: (bi, h // share, 0, 0))]
    args = [q, k, v]
    if lam_init is None:
        out_w = group * hd
    else:
        out_w = hd
        in_specs += [pl.BlockSpec(lam.shape, lambda bi, h, j: (0, 0)),
                     pl.BlockSpec((1, hd), lambda bi, h, j: (0, 0))]
        args += [lam, subln.reshape(1, hd)]
    return pl.pallas_call(
        kern, grid=(b, hkv, s // TM), in_specs=in_specs,
        out_specs=pl.BlockSpec((1, TM, out_w), lambda bi, h, j: (bi, j, h)),
        out_shape=jax.ShapeDtypeStruct((b, s, hkv * out_w), MXU_DTYPE),
        scratch_shapes=[pltpu.VMEM((group * TM, kv_chunk), F32), pltpu.VMEM((group * TM, kv_chunk), F32)],
        compiler_params=_params("arbitrary", "arbitrary", "arbitrary"), name="flash_attention",
    )(*args)


def _attn_out_body(xt, mod, ng, o_ref, wo_ref):
    return _dot(o_ref[0], wo_ref[...])


def _attn_out_layer(x, modt, ng, router, o, w_o):
    b, s, d = x.shape
    specs = [pl.BlockSpec((1, TM, d), lambda bi, j: (bi, j, 0)),
             pl.BlockSpec((d, d), lambda bi, j: (0, 0))]
    return _mixer_call(_attn_out_body, x, modt, ng, router, (o, w_o.astype(MXU_DTYPE)), specs, "attn_out")


def _rope_tables(ctx_len, lat_len, dim, reps):
    n = dim // 4
    pos = jnp.arange(lat_len, dtype=jnp.int32)
    row = (pos // GRID_W).astype(F32)
    col = (pos % GRID_W).astype(F32)
    freqs = ROPE_THETA ** (-jnp.arange(n, dtype=F32) / n)
    ang = jnp.concatenate([row[:, None] * freqs, col[:, None] * freqs], axis=-1)
    ang = jnp.concatenate([jnp.zeros((ctx_len, dim // 2), F32), ang], axis=0)
    cos, sin = jnp.cos(ang), jnp.sin(ang)
    return jnp.tile(cos, (1, 2 * reps)), jnp.tile(jnp.concatenate([-sin, sin], axis=-1), (1, reps))


N_ZERO_BLOCKS = 2 * N_EXPERTS
DISPATCH_BUFS = 3


def _dispatch_kernel(zero_ref, flag_ref, dest_ref, h_hbm, hs_hbm, zero_buf, tiles, zero_sem, load_sem, row_sem):
    i = pl.program_id(0)
    n = pl.num_programs(0)

    @pl.when(i == 0)
    def _():
        zero_buf[...] = jnp.zeros_like(zero_buf)

        def zero_copy(e):
            start = pl.multiple_of(zero_ref[e], SUBLANES)
            return pltpu.make_async_copy(zero_buf, hs_hbm.at[pl.ds(start, EXPERT_ROWS)], zero_sem)

        for e in range(N_ZERO_BLOCKS):
            pl.when(flag_ref[e] != 0)(lambda e=e: zero_copy(e).start())
        for e in range(N_ZERO_BLOCKS):
            pl.when(flag_ref[e] != 0)(lambda e=e: zero_copy(e).wait())

    def tile_load(t, slot):
        return pltpu.make_async_copy(h_hbm.at[pl.ds(pl.multiple_of(t * TM, TM), TM)], tiles.at[slot],
                                     load_sem.at[slot])

    def rows_done(slot):
        for _ in range(TOP_K):
            pltpu.make_async_copy(tiles.at[slot], tiles.at[slot], row_sem.at[slot]).wait()

    slot = i % DISPATCH_BUFS

    @pl.when(i == 0)
    def _():
        tile_load(0, 0).start()

    @pl.when(i + 1 < n)
    def _():
        tile_load(i + 1, (i + 1) % DISPATCH_BUFS).start()

    tile_load(i, slot).wait()
    for r in range(TM):
        for k in range(TOP_K):
            pltpu.make_async_copy(tiles.at[slot, pl.ds(r, 1)], hs_hbm.at[pl.ds(dest_ref[0, 0, r * TOP_K + k], 1)],
                                  row_sem.at[slot]).start(priority=k % 2)

    @pl.when(i >= 1)
    def _():
        rows_done((i + DISPATCH_BUFS - 1) % DISPATCH_BUFS)

    @pl.when(i == n - 1)
    def _():
        rows_done(slot)


def _dispatch(h2, dest3, zero_start, zero_flag, n_pad):
    t, d = h2.shape
    grid_spec = pltpu.PrefetchScalarGridSpec(
        num_scalar_prefetch=2, grid=(t // TM,),
        in_specs=[pl.BlockSpec((1, 1, TM * TOP_K), lambda i, zs, zf: (i, 0, 0), memory_space=pltpu.SMEM),
                  pl.BlockSpec(memory_space=pl.ANY)],
        out_specs=pl.BlockSpec(memory_space=pl.ANY),
        scratch_shapes=[pltpu.VMEM((EXPERT_ROWS, d), F32), pltpu.VMEM((DISPATCH_BUFS, TM, d), F32),
                        pltpu.SemaphoreType.DMA, pltpu.SemaphoreType.DMA((DISPATCH_BUFS,)),
                        pltpu.SemaphoreType.DMA((DISPATCH_BUFS,))])
    return pl.pallas_call(
        _dispatch_kernel, grid_spec=grid_spec,
        out_shape=jax.ShapeDtypeStruct((n_pad, d), F32),
        compiler_params=pltpu.CompilerParams(dimension_semantics=("arbitrary",), has_side_effects=True),
        name="moe_dispatch",
    )(zero_start, zero_flag, dest3, h2)


def _expert_kernel(be_ref, nused_ref, hs_ref, w1_ref, b1_ref, w2_ref, b2_ref, o_ref, w1_mxu, w2_mxu):
    i = pl.program_id(0)
    f = w2_ref.shape[1]

    @pl.when((i == 0) | (be_ref[i] != be_ref[jnp.maximum(i - 1, 0)]))
    def _():
        w1_mxu[...] = w1_ref[0].astype(w1_mxu.dtype)
        w2_mxu[...] = w2_ref[0].astype(w2_mxu.dtype)

    @pl.when(i < nused_ref[0])
    def _():
        gu = _dot(hs_ref[...], w1_mxu[...]) + b1_ref[0]
        g = jnp.minimum(gu[:, :f], SWIGLU_LIMIT)
        u = jnp.clip(gu[:, f:], -SWIGLU_LIMIT, SWIGLU_LIMIT)
        act = g * _sigmoid(SWIGLU_ALPHA * g) * (u + 1.0)
        o_ref[...] = _dot(act, w2_mxu[...]) + b2_ref[0]

    @pl.when(i >= nused_ref[0])
    def _():
        o_ref[...] = jnp.zeros_like(o_ref)


def _experts(hs, block_e, n_used, w1, b1, w2, b2, layer):
    n_pad, d = hs.shape
    _, e, _, f2 = w1.shape
    f = w2.shape[2]
    tb = EXPERT_ROWS
    grid_spec = pltpu.PrefetchScalarGridSpec(
        num_scalar_prefetch=2, grid=(n_pad // tb,),
        in_specs=[pl.BlockSpec((tb, d), lambda i, be, nu: (i, 0)),
                  pl.BlockSpec((None, 1, d, f2), lambda i, be, nu: (layer, be[i], 0, 0)),
                  pl.BlockSpec((1, 1, f2), lambda i, be, nu: (be[i], 0, 0)),
                  pl.BlockSpec((None, 1, f, d), lambda i, be, nu: (layer, be[i], 0, 0)),
                  pl.BlockSpec((1, 1, d), lambda i, be, nu: (be[i], 0, 0))],
        out_specs=pl.BlockSpec((tb, d), lambda i, be, nu: (i, 0)),
        scratch_shapes=[pltpu.VMEM((d, f2), MXU_DTYPE), pltpu.VMEM((f, d), MXU_DTYPE)])
    return pl.pallas_call(
        _expert_kernel, grid_spec=grid_spec,
        out_shape=jax.ShapeDtypeStruct((n_pad, d), F32),
        compiler_params=_params("arbitrary"), name="moe_experts",
    )(block_e, n_used, hs, w1, b1.reshape(e, 1, f2), w2, b2.reshape(e, 1, d))


def _combine_kernel(dest_ref, dest_next_ref, ys_hbm, mf_ref, x1_ref, mod_ref, ng_ref, x2_ref, buf, sem):
    tile = pl.program_id(0) * pl.num_programs(1) + pl.program_id(1)
    n_tiles = pl.num_programs(0) * pl.num_programs(1)
    slot = tile % 2

    def fetch(idx_ref, to_slot):
        for r in range(TM):
            for k in range(TOP_K):
                pltpu.make_async_copy(ys_hbm.at[pl.ds(idx_ref[0, 0, r * TOP_K + k], 1)],
                                      buf.at[to_slot, k, pl.ds(r, 1)], sem.at[to_slot]).start(priority=k % 2)

    @pl.when(tile == 0)
    def _():
        fetch(dest_ref, 0)

    @pl.when(tile + 1 < n_tiles)
    def _():
        fetch(dest_next_ref, 1 - slot)

    pltpu.make_async_copy(buf.at[slot], buf.at[slot], sem.at[slot]).wait()
    gates = mf_ref[...]
    y = gates[:, 0:1] * buf[slot, 0]
    for k in range(1, TOP_K):
        y = y + gates[:, k:k + 1] * buf[slot, k]
    mod = mod_ref[0]
    x2_ref[0] = x1_ref[0] + mod[5:6] * _rms(y, ng_ref[3:4])


def _combine(ys, dest3, mf, x1, modt, ng, latent_only):
    b, s, d = x1.shape
    nt = s // TM
    if latent_only:
        out_spec = pl.BlockSpec((1, TM, d), lambda bi, j: (bi, jnp.maximum(j - 1, 0), 0))
        out_shape = jax.ShapeDtypeStruct((b, s - TM, d), F32)
    else:
        out_spec = pl.BlockSpec((1, TM, d), lambda bi, j: (bi, j, 0))
        out_shape = jax.ShapeDtypeStruct((b, s, d), F32)
    return pl.pallas_call(
        _combine_kernel, grid=(b, nt),
        in_specs=[pl.BlockSpec((1, 1, TM * TOP_K), lambda bi, j: (bi * nt + j, 0, 0), memory_space=pltpu.SMEM),
                  pl.BlockSpec((1, 1, TM * TOP_K), lambda bi, j: (jnp.minimum(bi * nt + j + 1, b * nt - 1), 0, 0),
                               memory_space=pltpu.SMEM),
                  pl.BlockSpec(memory_space=pl.ANY),
                  pl.BlockSpec((TM, LANES), lambda bi, j: (bi * nt + j, 0)),
                  pl.BlockSpec((1, TM, d), lambda bi, j: (bi, j, 0)),
                  pl.BlockSpec((1, SUBLANES, d), lambda bi, j: (bi * 2 + jnp.minimum(j, 1), 0, 0)),
                  pl.BlockSpec((4, d), lambda bi, j: (0, 0))],
        out_specs=out_spec, out_shape=out_shape,
        scratch_shapes=[pltpu.VMEM((2, TOP_K, TM, d), F32), pltpu.SemaphoreType.DMA((2,))],
        compiler_params=_params("arbitrary", "arbitrary"), name="moe_combine",
    )(dest3, dest3, ys, mf, x1, modt, ng)


def _moe(x1, h2, mi, mf, cnt, modt, ng, w1, b1, w2, b2, layer, latent_only):
    t = h2.shape[0]
    tb = EXPERT_ROWS
    counts = cnt[0, :N_EXPERTS].astype(jnp.int32)
    padded = (counts + tb - 1) // tb * tb
    pend = jnp.cumsum(padded)
    pstart = pend - padded
    dest = pstart[mi[:, :TOP_K]] + mi[:, TOP_K:2 * TOP_K]
    n_blocks = -(-(t * TOP_K) // tb) + N_EXPERTS
    block_row0 = jnp.arange(n_blocks, dtype=jnp.int32) * tb
    block_e = jnp.minimum(jnp.sum((pend[None, :] <= block_row0[:, None]).astype(jnp.int32), axis=1),
                          N_EXPERTS - 1)
    n_used = (pend[-1:] // tb).astype(jnp.int32)
    dest3 = dest.reshape(t // TM, 1, TM * TOP_K)
    tail_start = jnp.maximum(pend - tb, 0).astype(jnp.int32)
    trailing = block_row0[n_blocks - N_EXPERTS:]
    zero_start = jnp.concatenate([tail_start, trailing])
    zero_flag = jnp.concatenate([padded > 0, trailing >= pend[-1]]).astype(jnp.int32)
    hs = _dispatch(h2, dest3, zero_start, zero_flag, n_blocks * tb)
    ys = _experts(hs, block_e, n_used, w1, b1, w2, b2, layer)
    return _combine(ys, dest3, mf, x1, modt, ng, latent_only)


def kernel(x, c, ctx, c_ctx, mod_w, mod_b, norm_g, pool_w, pool_scale, gqa_w_qkv, gqa_w_o, gqa_q_gain, gqa_k_gain, conv_w_in, conv_w_dw, conv_w_out, diff_w_qkv, diff_w_o, diff_lambda, diff_subln_gain, router_w, router_b, moe_w1, moe_b1, moe_w2, moe_b2):
    b, lat_len, d = x.shape
    ctx_len = ctx.shape[1]
    depth = mod_w.shape[0]
    n_mixers = 4
    assert ctx_len == TM and lat_len % TM == 0 and b + 1 <= SUBLANES and d % LANES == 0

    xa = jnp.concatenate([ctx, x], axis=1)

    cv = jnp.zeros((SUBLANES, d), F32).at[:b].set(c).at[b].set(c_ctx)
    mods = _modulation(cv, mod_w, mod_b).reshape(depth, SUBLANES, MOD_CHUNKS, d)
    modt_all = jnp.stack([jnp.broadcast_to(mods[:, b][:, None], (depth, b, MOD_CHUNKS, d)), mods[:, :b]], axis=2)
    modt_all = jnp.pad(modt_all, ((0, 0), (0, 0), (0, 0), (0, SUBLANES - MOD_CHUNKS), (0, 0)))
    modt_all = modt_all.reshape(depth, b * 2, SUBLANES, d)

    cos_b, sin_b = _rope_tables(ctx_len, lat_len, GQA_HEAD_DIM, 1)
    cos_d, sin_d = _rope_tables(ctx_len, lat_len, DIFF_HEAD_DIM, 2)

    for i in range(depth):
        m, jj = i % n_mixers, i // n_mixers
        modt, ng = modt_all[i], norm_g[i]
        rw = jnp.zeros((d, LANES), F32).at[:, :N_EXPERTS].set(router_w[i])
        rwh = rw.astype(jnp.bfloat16)
        rwl = (rw - rwh.astype(F32)).astype(jnp.bfloat16)
        rb = jnp.full((1, LANES), NEG_BIG, F32).at[0, :N_EXPERTS].set(router_b[i])
        router = (rwh, rwl, rb)
        if m == 0:
            res = _pool_layer(xa, modt, ng, router, pool_w[jj], pool_scale[jj], ctx_len)
        elif m == 1:
            q, k, v = _gqa_qkv(xa, modt, ng, gqa_w_qkv[jj], gqa_q_gain[jj], gqa_k_gain[jj], cos_b, sin_b)
            o = _flash(q, k, v, GQA_STACK, ctx_len)
            res = _attn_out_layer(xa, modt, ng, router, o, gqa_w_o[jj])
        elif m == 2:
            res = _conv_layer(xa, modt, ng, router, conv_w_in[jj], conv_w_dw[jj], conv_w_out[jj])
        else:
            lam_init = 0.8 - 0.6 * math.exp(-0.3 * i)
            q, k, v = _diff_qkv(xa, modt, ng, diff_w_qkv[jj], cos_d, sin_d)
            o = _flash(q, k, v, 2, ctx_len, lam=diff_lambda[jj], subln=diff_subln_gain[jj], lam_init=lam_init)
            res = _attn_out_layer(xa, modt, ng, router, o, diff_w_o[jj])
        x1, h2, mi, mf, cnt = res
        xa = _moe(x1, h2, mi, mf, cnt, modt, ng,
                  moe_w1, moe_b1[i], moe_w2, moe_b2[i], layer=i, latent_only=(i == depth - 1))
    return xa
```
